```python
import jax, jax.numpy as jnp
from jax import lax
import numpy as np

D_MODEL = 2048
BATCH = 4
SEQ = 4096
DEPTH = 4

N_A = DEPTH // 2
N_B = DEPTH - N_A
D_PLE = 256
D_FF = 5632
HEAD_DIM = 64
N_HEADS_A = D_MODEL // HEAD_DIM
LORA_DECAY = 96
LORA_A = 96
LORA_V = 64
LORA_G = 256
N_Q_HEADS = D_MODEL // HEAD_DIM
N_KV_HEADS = 8
GROUP = N_Q_HEADS // N_KV_HEADS
WINDOW = 128
BLOCK = 128
ROPE_THETA = 10000.0
RMS_EPS = 1e-6
GN_EPS = 64e-5

kernel_name = "yoco_rwkv7_swa_sink_macaron_ple"


def rms_norm(x, g):
    xf = x.astype(jnp.float32)
    y = xf * lax.rsqrt(jnp.mean(xf * xf, axis=-1, keepdims=True) + RMS_EPS)
    return (y * g.astype(jnp.float32)).astype(x.dtype)


def swiglu(h, w_gate, w_up, w_down):
    return (jax.nn.silu(h @ w_gate) * (h @ w_up)) @ w_down


def rope_tables(positions):
    inv_freq = ROPE_THETA ** (-jnp.arange(0, HEAD_DIM, 2, dtype=jnp.float32) / HEAD_DIM)
    ang = positions.astype(jnp.float32)[..., None] * inv_freq
    return jnp.cos(ang)[:, :, None, :], jnp.sin(ang)[:, :, None, :]


def apply_rope(t, cos, sin):
    tf = t.astype(jnp.float32)
    t1, t2 = tf[..., : HEAD_DIM // 2], tf[..., HEAD_DIM // 2:]
    out = jnp.concatenate([t1 * cos - t2 * sin, t2 * cos + t1 * sin], axis=-1)
    return out.astype(t.dtype)


def wkv7_scan(r, w, k, v, a, b):
    def step(state, inp):
        r_t, w_t, k_t, v_t, a_t, b_t = inp
        sa = jnp.einsum('bhij,bhj->bhi', state, a_t)
        state = (state * w_t[:, :, None, :] + sa[..., None] * b_t[:, :, None, :]
                 + v_t[..., None] * k_t[:, :, None, :])
        return state, jnp.einsum('bhij,bhj->bhi', state, r_t)
    xs = tuple(jnp.swapaxes(t, 0, 1) for t in (r, w, k, v, a, b))
    state0 = jnp.zeros((r.shape[0], r.shape[2], HEAD_DIM, HEAD_DIM), jnp.float32)
    _, y = lax.scan(step, state0, xs)
    return jnp.swapaxes(y, 0, 1)


def rwkv7_time_mix(h, v_first, mu, w_r, w_k, w_v, w_o, w0, w1, w2, a0, a1, a2,
                   g1, g2, k_k, k_a, r_k, gn_g, gn_b, vmix):
    bsz, seq, _ = h.shape
    f32 = jnp.float32
    h_prev = jnp.pad(h, ((0, 0), (1, 0), (0, 0)))[:, :-1]
    dx = h_prev - h
    xr, xw, xk, xv, xa, xg = (h + dx * mu[c] for c in range(6))
    r = xr @ w_r
    k = xk @ w_k
    v = xv @ w_v
    w_log = -jax.nn.softplus(-(w0 + jnp.tanh(xw @ w1) @ w2).astype(f32)) - 0.5
    decay = jnp.exp(-jnp.exp(w_log))
    a = jax.nn.sigmoid((a0 + (xa @ a1) @ a2).astype(f32))
    g = jax.nn.sigmoid(xg @ g1) @ g2
    if vmix is None:
        v_first = v
    else:
        v0, v1, v2 = vmix
        v = v + (v_first - v) * jax.nn.sigmoid(v0 + (xv @ v1) @ v2)

    def heads(t):
        return t.astype(f32).reshape(bsz, seq, N_HEADS_A, HEAD_DIM)

    kk = heads(k * k_k)
    kk = kk / jnp.maximum(jnp.sqrt(jnp.sum(kk * kk, axis=-1, keepdims=True)), 1e-12)
    a_h = heads(a)
    k_h = heads(k.astype(f32) * (1.0 + (a - 1.0) * k_a.astype(f32)))
    r_h = heads(r)
    v_h = heads(v)
    y = wkv7_scan(r_h, heads(decay), k_h, v_h, -kk, kk * a_h)
    mean = jnp.mean(y, axis=-1, keepdims=True)
    var = jnp.mean(jnp.square(y - mean), axis=-1, keepdims=True)
    y = ((y - mean) * lax.rsqrt(var + GN_EPS)).reshape(bsz, seq, D_MODEL)
    y = y * gn_g.astype(f32) + gn_b.astype(f32)
    bonus = jnp.sum(r_h * k_h * r_k.astype(f32), axis=-1, keepdims=True) * v_h
    y = y + bonus.reshape(bsz, seq, D_MODEL)
    out = (y.astype(h.dtype) * g) @ w_o
    return out, v_first


def to_band(t):
    bsz, seq = t.shape[0], t.shape[1]
    tb = t.reshape(bsz, seq // BLOCK, BLOCK, N_KV_HEADS, HEAD_DIM)
    prev = jnp.pad(tb, ((0, 0), (1, 0), (0, 0), (0, 0), (0, 0)))[:, :-1]
    return jnp.concatenate([prev, tb], axis=2)


def band_mask(n_blk):
    qi = jnp.arange(BLOCK)[None, :, None]
    ki = jnp.arange(2 * BLOCK)[None, None, :]
    blk = jnp.arange(n_blk)[:, None, None]
    rel = BLOCK + qi - ki
    return (rel >= 0) & (rel < WINDOW) & (blk * BLOCK + ki - BLOCK >= 0)


def swa_sink_attention(q, k_band, v_band, sinks, mask):
    bsz, seq = q.shape[0], q.shape[1]
    n_blk = seq // BLOCK
    qb = q.reshape(bsz, n_blk, BLOCK, N_KV_HEADS, GROUP, HEAD_DIM)
    scores = jnp.einsum('bnqhgd,bnshd->bnhgqs', qb, k_band).astype(jnp.float32) * (HEAD_DIM ** -0.5)
    scores = jnp.where(mask[None, :, None, None], scores, -jnp.inf)
    sink = sinks.astype(jnp.float32).reshape(N_KV_HEADS, GROUP)[None, None, :, :, None, None]
    m = jnp.maximum(jnp.max(scores, axis=-1, keepdims=True), sink)
    e = jnp.exp(scores - m)
    probs = e / (jnp.sum(e, axis=-1, keepdims=True) + jnp.exp(sink - m))
    out = jnp.einsum('bnhgqs,bnshd->bnqhgd', probs.astype(v_band.dtype), v_band)
    return out.reshape(bsz, seq, N_Q_HEADS * HEAD_DIM)


def setup_inputs(seed: int = 0) -> dict:
    key = jax.random.key(seed)
    ks = iter(jax.random.split(key, 48))
    nrm = lambda shape, s: jax.random.normal(next(ks), shape, jnp.float32) * s
    D, F, H, N = D_MODEL, D_FF, N_HEADS_A, HEAD_DIM
    start = jax.random.randint(next(ks), (BATCH, 1), 0, 1024, dtype=jnp.int32)
    positions = (start + jnp.arange(SEQ, dtype=jnp.int32)[None, :]).astype(jnp.int32)
    return {
        "x": nrm((BATCH, SEQ, D), 1.0),
        "p": nrm((DEPTH, BATCH, SEQ, D_PLE), 1.0),
        "positions": positions,
        "norm_g": 1.0 + nrm((DEPTH, 4, D), 0.1),
        "ffn_w_gate": nrm((DEPTH, 2, D, F), D ** -0.5),
        "ffn_w_up": nrm((DEPTH, 2, D, F), D ** -0.5),
        "ffn_w_down": nrm((DEPTH, 2, F, D), F ** -0.5),
        "ple_w_up": nrm((DEPTH, D_PLE, D), D_PLE ** -0.5),
        "ple_w_gate": nrm((DEPTH, D, D), D ** -0.5),
        "rwkv_mu": jax.random.uniform(next(ks), (N_A, 6, D), jnp.float32),
        "rwkv_w_rkv": nrm((N_A, 3, D, D), D ** -0.5),
        "rwkv_w_o": nrm((N_A, D, D), D ** -0.5),
        "rwkv_w0": jax.random.uniform(next(ks), (N_A, D), jnp.float32, -6.0, -1.0),
        "rwkv_w1": nrm((N_A, D, LORA_DECAY), D ** -0.5),
        "rwkv_w2": nrm((N_A, LORA_DECAY, D), 0.5 * LORA_DECAY ** -0.5),
        "rwkv_a0": nrm((N_A, D), 0.1),
        "rwkv_a1": nrm((N_A, D, LORA_A), D ** -0.5),
        "rwkv_a2": nrm((N_A, LORA_A, D), 0.5 * LORA_A ** -0.5),
        "rwkv_v0": nrm((N_A - 1, D), 0.1),
        "rwkv_v1": nrm((N_A - 1, D, LORA_V), D ** -0.5),
        "rwkv_v2": nrm((N_A - 1, LORA_V, D), 0.5 * LORA_V ** -0.5),
        "rwkv_g1": nrm((N_A, D, LORA_G), D ** -0.5),
        "rwkv_g2": nrm((N_A, LORA_G, D), LORA_G ** -0.5),
        "rwkv_k_k": 0.85 + nrm((N_A, D), 0.1),
        "rwkv_k_a": 1.0 + nrm((N_A, D), 0.1),
        "rwkv_r_k": nrm((N_A, H, N), 0.1),
        "rwkv_gn_g": 1.0 + nrm((N_A, D), 0.1),
        "rwkv_gn_b": nrm((N_A, D), 0.02),
        "kv_norm_g": 1.0 + nrm((D,), 0.1),
        "w_kv": nrm((D, 2 * N_KV_HEADS * HEAD_DIM), D ** -0.5),
        "attn_w_q": nrm((N_B, D, N_Q_HEADS * HEAD_DIM), D ** -0.5),
        "attn_w_o": nrm((N_B, N_Q_HEADS * HEAD_DIM, D), D ** -0.5),
        "attn_sinks": nrm((N_B, N_Q_HEADS), 1.0),
        "final_norm_g": 1.0 + nrm((D,), 0.1),
    }


def reference(x, p, positions, norm_g, ffn_w_gate, ffn_w_up, ffn_w_down, ple_w_up, ple_w_gate,
              rwkv_mu, rwkv_w_rkv, rwkv_w_o, rwkv_w0, rwkv_w1, rwkv_w2, rwkv_a0, rwkv_a1, rwkv_a2,
              rwkv_v0, rwkv_v1, rwkv_v2, rwkv_g1, rwkv_g2, rwkv_k_k, rwkv_k_a, rwkv_r_k,
              rwkv_gn_g, rwkv_gn_b, kv_norm_g, w_kv, attn_w_q, attn_w_o, attn_sinks, final_norm_g):
    bsz, seq, _ = x.shape
    cos, sin = rope_tables(positions)
    mask = band_mask(seq // BLOCK)
    v_first = None
    k_band = None
    v_band = None
    for i in range(DEPTH):
        if i == N_A:
            kv = rms_norm(x, kv_norm_g) @ w_kv
            k_sh, v_sh = jnp.split(kv.reshape(bsz, seq, 2 * N_KV_HEADS, HEAD_DIM), 2, axis=2)
            k_band = to_band(apply_rope(k_sh, cos, sin))
            v_band = to_band(v_sh)
        x = x + 0.5 * swiglu(rms_norm(x, norm_g[i, 0]), ffn_w_gate[i, 0], ffn_w_up[i, 0], ffn_w_down[i, 0])
        h = rms_norm(x, norm_g[i, 1])
        if i < N_A:
            j = i
            vmix = None if j == 0 else (rwkv_v0[j - 1], rwkv_v1[j - 1], rwkv_v2[j - 1])
            mix, v_first = rwkv7_time_mix(
                h, v_first, rwkv_mu[j], rwkv_w_rkv[j, 0], rwkv_w_rkv[j, 1], rwkv_w_rkv[j, 2], rwkv_w_o[j],
                rwkv_w0[j], rwkv_w1[j], rwkv_w2[j], rwkv_a0[j], rwkv_a1[j], rwkv_a2[j],
                rwkv_g1[j], rwkv_g2[j], rwkv_k_k[j], rwkv_k_a[j], rwkv_r_k[j],
                rwkv_gn_g[j], rwkv_gn_b[j], vmix)
        else:
            j = i - N_A
            q = apply_rope((h @ attn_w_q[j]).reshape(bsz, seq, N_Q_HEADS, HEAD_DIM), cos, sin)
            mix = swa_sink_attention(q, k_band, v_band, attn_sinks[j], mask) @ attn_w_o[j]
        x = x + mix
        x = x + 0.5 * swiglu(rms_norm(x, norm_g[i, 2]), ffn_w_gate[i, 1], ffn_w_up[i, 1], ffn_w_down[i, 1])
        gate = jax.nn.sigmoid(rms_norm(x, norm_g[i, 3]) @ ple_w_gate[i])
        x = x + gate * (p[i] @ ple_w_up[i])
    return rms_norm(x, final_norm_g)
```

```python
import functools
import math

import jax
import jax.numpy as jnp
from jax import lax
from jax.experimental import pallas as pl
from jax.experimental.pallas import tpu as pltpu

F32 = jnp.float32
BF16 = jnp.bfloat16

HEAD_DIM = 64
LANES = 128
CHUNK = 64
WINDOW = 128
GROUP = 4
ROPE_THETA = 10000.0
RMS_EPS = 1e-6
GN_EPS = 64e-5
DECAY_SCALE = math.exp(-0.5)
VMEM_LIMIT = 56 * 1024 * 1024


def _params(n_axes):
    return pltpu.CompilerParams(dimension_semantics=("arbitrary",) * n_axes,
                                vmem_limit_bytes=VMEM_LIMIT)


def _rms(x, g):
    return x * lax.rsqrt(jnp.mean(x * x, axis=-1, keepdims=True) + RMS_EPS) * g


def _sigmoid(x):
    return 1.0 / (1.0 + jnp.exp(-x))


def _dot(a, b):
    return jnp.dot(a, b, preferred_element_type=F32)


def _dot_nt(a, b):
    return lax.dot_general(a, b, (((1,), (1,)), ((), ())), preferred_element_type=F32)


def _dot_tn(a, b):
    return lax.dot_general(a, b, (((0,), (0,)), ((), ())), preferred_element_type=F32)


def _fmm_kernel(*refs, n_row, n_extra, n_out, prologue, epilogue):
    row = refs[:n_row]
    w_ref = refs[n_row]
    extra = refs[n_row + 1:n_row + 1 + n_extra]
    outs = refs[n_row + 1 + n_extra:n_row + 1 + n_extra + n_out]
    lhs_ref = refs[-1]

    @pl.when(pl.program_id(1) == 0)
    def _():
        lhs_ref[...] = prologue(*[r[...] for r in row]).astype(BF16)

    acc = _dot(lhs_ref[...], w_ref[...])
    res = epilogue(acc, *[e[...] for e in extra])
    if n_out == 1:
        res = (res,)
    for o, v in zip(outs, res):
        o[...] = v


def fused_matmul(rows, w, extras, prologue, epilogue, n_out, out_width, *, tm, tn, name):
    t = max(r.shape[0] for r in rows)
    k, n = w.shape
    grid = (t // tm, n // tn)
    in_specs = []
    for r in rows:
        if r.shape[0] == 1:
            in_specs.append(pl.BlockSpec((1, r.shape[1]), lambda i, j: (0, 0)))
        else:
            in_specs.append(pl.BlockSpec((tm, r.shape[1]), lambda i, j: (i, 0)))
    if tn == n:
        in_specs.append(pl.BlockSpec((k, tn), lambda i, j: (0, 0), pipeline_mode=pl.Buffered(1)))
    else:
        in_specs.append(pl.BlockSpec((k, tn), lambda i, j: (0, j)))
    for arr, bs, im in extras:
        in_specs.append(pl.BlockSpec(bs, im))
    otn = out_width // (n // tn)
    out_shape = [jax.ShapeDtypeStruct((t, out_width), F32)] * n_out
    out_specs = [pl.BlockSpec((tm, otn), lambda i, j: (i, j))] * n_out
    res = pl.pallas_call(
        functools.partial(_fmm_kernel, n_row=len(rows), n_extra=len(extras), n_out=n_out,
                          prologue=prologue, epilogue=epilogue),
        out_shape=out_shape, grid=grid, in_specs=in_specs, out_specs=out_specs,
        scratch_shapes=[pltpu.VMEM((tm, k), BF16)],
        compiler_params=_params(2), name=name,
    )(*rows, w, *[e[0] for e in extras])
    return res[0] if n_out == 1 else res


def _ffn_kernel(x_ref, g_ref, wg_ref, wu_ref, wd_ref, o_ref, h_ref):
    f = pl.program_id(1)

    @pl.when(f == 0)
    def _():
        h_ref[...] = _rms(x_ref[...], g_ref[...]).astype(BF16)

    h = h_ref[...]
    gate = _dot(h, wg_ref[...])
    up = _dot(h, wu_ref[...])
    act = (gate * _sigmoid(gate) * up).astype(BF16)
    part = _dot(act, wd_ref[...])

    @pl.when(f == 0)
    def _():
        o_ref[...] = part

    @pl.when(f > 0)
    def _():
        o_ref[...] += part

    @pl.when(f == pl.num_programs(1) - 1)
    def _():
        o_ref[...] = x_ref[...] + 0.5 * o_ref[...]


def ffn_half_step(x, g, wg, wu, wd, *, tm=512, tf=512):
    t, d = x.shape
    f = wg.shape[1]
    return pl.pallas_call(
        _ffn_kernel,
        out_shape=jax.ShapeDtypeStruct((t, d), F32),
        grid=(t // tm, f // tf),
        in_specs=[pl.BlockSpec((tm, d), lambda i, j: (i, 0)),
                  pl.BlockSpec((1, d), lambda i, j: (0, 0)),
                  pl.BlockSpec((d, tf), lambda i, j: (0, j)),
                  pl.BlockSpec((d, tf), lambda i, j: (0, j)),
                  pl.BlockSpec((tf, d), lambda i, j: (j, 0))],
        out_specs=pl.BlockSpec((tm, d), lambda i, j: (i, 0)),
        scratch_shapes=[pltpu.VMEM((tm, d), BF16)],
        compiler_params=_params(2), name="ffn_half_step",
    )(x, g, wg, wu, wd)


def _norm_shift_kernel(x_ref, g_ref, h_ref, dx_ref, carry_ref, *, tiles_per_seq):
    i = pl.program_id(0)

    @pl.when(i == 0)
    def _():
        carry_ref[...] = jnp.zeros_like(carry_ref)

    h = _rms(x_ref[...], g_ref[...])
    tm = h.shape[0]
    last = carry_ref[7:8, :]
    first = jnp.where(i % tiles_per_seq == 0, jnp.zeros_like(last), last)
    rolled = pltpu.roll(h, 1, axis=0)
    row = lax.broadcasted_iota(jnp.int32, h.shape, 0)
    h_prev = jnp.where(row == 0, first, rolled)
    h_ref[...] = h
    dx_ref[...] = h_prev - h
    carry_ref[...] = h[tm - 8:, :]


def norm_shift(x, g, seq, *, tm=512):
    t, d = x.shape
    return pl.pallas_call(
        functools.partial(_norm_shift_kernel, tiles_per_seq=seq // tm),
        out_shape=[jax.ShapeDtypeStruct((t, d), F32)] * 2,
        grid=(t // tm,),
        in_specs=[pl.BlockSpec((tm, d), lambda i: (i, 0)),
                  pl.BlockSpec((1, d), lambda i: (0, 0))],
        out_specs=[pl.BlockSpec((tm, d), lambda i: (i, 0))] * 2,
        scratch_shapes=[pltpu.VMEM((8, d), F32)],
        compiler_params=_params(1), name="norm_shift",
    )(x, g)


def _lora_kernel(h_ref, dx_ref, mu_ref, w1_ref, w2_ref, b_ref, o_ref, *, mid_act, out_act):
    xm = (h_ref[...] + dx_ref[...] * mu_ref[...]).astype(BF16)
    mid = mid_act(_dot(xm, w1_ref[...]))
    z = _dot(mid.astype(BF16), w2_ref[...]) + b_ref[...]
    o_ref[...] = out_act(z)


def lora(h, dx, mu, w1, w2, bias, mid_act, out_act, *, tm=512, name):
    t, d = h.shape
    r = w1.shape[1]
    rp = -(-r // LANES) * LANES
    w1 = jnp.pad(w1, ((0, 0), (0, rp - r))).astype(BF16)
    w2 = jnp.pad(w2, ((0, rp - r), (0, 0))).astype(BF16)
    n = w2.shape[1]
    return pl.pallas_call(
        functools.partial(_lora_kernel, mid_act=mid_act, out_act=out_act),
        out_shape=jax.ShapeDtypeStruct((t, n), F32),
        grid=(t // tm,),
        in_specs=[pl.BlockSpec((tm, d), lambda i: (i, 0)),
                  pl.BlockSpec((tm, d), lambda i: (i, 0)),
                  pl.BlockSpec((1, d), lambda i: (0, 0)),
                  pl.BlockSpec((d, rp), lambda i: (0, 0)),
                  pl.BlockSpec((rp, n), lambda i: (0, 0)),
                  pl.BlockSpec((1, n), lambda i: (0, 0))],
        out_specs=pl.BlockSpec((tm, n), lambda i: (i, 0)),
        compiler_params=_params(1), name=name,
    )(h, dx, mu, w1, w2, bias)


def _wkv_kernel(*refs, n_pairs, has_vmix):
    if has_vmix:
        (r_ref, k_ref, v_ref, lw_ref, a_ref, vf_ref, vg_ref,
         kk_ref, ka_ref, rk_ref, gg_ref, gb_ref, y_ref, s_ref) = refs
    else:
        (r_ref, k_ref, v_ref, lw_ref, a_ref,
         kk_ref, ka_ref, rk_ref, gg_ref, gb_ref, y_ref, s_ref) = refs

    @pl.when(pl.program_id(2) == 0)
    def _():
        s_ref[...] = jnp.zeros_like(s_ref)

    L = CHUNK
    row = lax.broadcasted_iota(jnp.int32, (L, LANES), 0)
    col = lax.broadcasted_iota(jnp.int32, (L, LANES), 1)
    strict = (col % L) < row
    incl = (col % L) <= row
    tri = (lax.broadcasted_iota(jnp.int32, (L, L), 1)
           <= lax.broadcasted_iota(jnp.int32, (L, L), 0)).astype(BF16)
    brow = lax.broadcasted_iota(jnp.int32, (LANES, LANES), 0)
    bcol = lax.broadcasted_iota(jnp.int32, (LANES, LANES), 1)
    same_head = (brow // HEAD_DIM) == (bcol // HEAD_DIM)
    ones_bd = same_head.astype(F32)
    head0 = col < HEAD_DIM

    def bd(x):
        return jnp.where(same_head, jnp.concatenate([x, x], axis=0), 0.0)

    for p in range(n_pairs):
        sl = (slice(None), slice(p * LANES, (p + 1) * LANES))
        r = r_ref[sl]
        k = k_ref[sl]
        v = v_ref[sl]
        lw = lw_ref[sl]
        a = a_ref[sl]
        if has_vmix:
            v = v + (vf_ref[sl] - v) * vg_ref[sl]
        kx = k * kk_ref[sl]
        ss = _dot(kx * kx, ones_bd)
        kk = kx * lax.rsqrt(jnp.maximum(ss, 1e-24))
        kh = k * (1.0 + (a - 1.0) * ka_ref[sl])
        a_s = -kk
        b_s = kk * a

        lw_hi = lw.astype(BF16)
        lw_lo = (lw - lw_hi.astype(F32)).astype(BF16)
        cum = _dot(tri, lw_hi) + _dot(tri, lw_lo)
        cum_last = cum[L - 1:L, :]
        e_fwd = jnp.exp(cum)
        e_inv = jnp.exp(-cum)
        e_tail = jnp.exp(cum_last - cum)
        r_t = r * e_fwd
        a_t = a_s * jnp.exp(cum - lw)
        b_t = b_s * e_inv
        k_t = kh * e_inv
        b_hat = b_s * e_tail
        k_hat = kh * e_tail

        lhs = jnp.concatenate([a_t, r_t], axis=0)
        rhs = jnp.concatenate([jnp.where(head0, b_t, 0.0), jnp.where(head0, 0.0, b_t),
                               jnp.where(head0, k_t, 0.0), jnp.where(head0, 0.0, k_t)], axis=0)
        a_all = _dot_nt(lhs, rhs)
        a_ab = jnp.where(strict, a_all[:L, :LANES], 0.0)
        a_ak = jnp.where(strict, a_all[:L, LANES:], 0.0)
        a_rb = jnp.where(incl, a_all[L:, :LANES], 0.0)
        a_rk = jnp.where(incl, a_all[L:, LANES:], 0.0)

        s = s_ref[p]
        from_state = _dot_nt(lhs, s)
        v_bd = bd(v)
        x = from_state[:L] + _dot(a_ak, v_bd)
        pw = a_ab
        n_iter = int(math.log2(L))
        for it in range(n_iter):
            x = x + _dot(pw, bd(x))
            if it < n_iter - 1:
                pw = _dot(pw, bd(pw))
        u = x
        y = from_state[L:] + _dot(a_rb, bd(u)) + _dot(a_rk, v_bd)

        upd = _dot_tn(jnp.concatenate([u, v], axis=0), jnp.concatenate([b_hat, k_hat], axis=0))
        s_ref[p] = s * jnp.exp(cum_last) + jnp.where(same_head, upd, 0.0)

        mean = _dot(y, ones_bd) * (1.0 / HEAD_DIM)
        dev = y - mean
        var = _dot(dev * dev, ones_bd) * (1.0 / HEAD_DIM)
        yn = dev * lax.rsqrt(var + GN_EPS) * gg_ref[sl] + gb_ref[sl]
        bonus = _dot(r * kh * rk_ref[sl], ones_bd) * v
        y_ref[sl] = yn + bonus


def wkv7(r, k, v, lw, a, vmix, k_k, k_a, r_k, gn_g, gn_b, seq, *, n_pairs=4):
    t, d = r.shape
    gw = n_pairs * LANES
    grid = (t // seq, d // gw, seq // CHUNK)
    n_chunks = seq // CHUNK
    tok = pl.BlockSpec((CHUNK, gw), lambda b, g, c: (b * n_chunks + c, g))
    vec = pl.BlockSpec((1, gw), lambda b, g, c: (0, g))
    toks = [r, k, v, lw, a] + (list(vmix) if vmix is not None else [])
    vecs = [k_k, k_a, r_k, gn_g, gn_b]
    return pl.pallas_call(
        functools.partial(_wkv_kernel, n_pairs=n_pairs, has_vmix=vmix is not None),
        out_shape=jax.ShapeDtypeStruct((t, d), F32),
        grid=grid,
        in_specs=[tok] * len(toks) + [vec] * len(vecs),
        out_specs=tok,
        scratch_shapes=[pltpu.VMEM((n_pairs, LANES, LANES), F32)],
        compiler_params=_params(3), name="wkv7",
    )(*toks, *vecs)


def _swa_kernel(q_ref, kp_ref, kc_ref, vp_ref, vc_ref, sink_ref, o_ref, *, blocks_per_seq, n_kv):
    n = pl.program_id(0) % blocks_per_seq
    w = WINDOW
    rows = GROUP * w
    qi = lax.broadcasted_iota(jnp.int32, (rows, 2 * w), 0) % w
    ki = lax.broadcasted_iota(jnp.int32, (rows, 2 * w), 1)
    rel = w + qi - ki
    valid = (rel >= 0) & (rel < w) & ((ki >= w) | (n > 0))
    for h in range(n_kv):
        ks = slice(h * HEAD_DIM, (h + 1) * HEAD_DIM)
        kb = jnp.concatenate([kp_ref[:, ks], kc_ref[:, ks]], axis=0)
        vb = jnp.concatenate([vp_ref[:, ks], vc_ref[:, ks]], axis=0)
        q4 = jnp.concatenate(
            [q_ref[:, (h * GROUP + g) * HEAD_DIM:(h * GROUP + g + 1) * HEAD_DIM]
             for g in range(GROUP)], axis=0)
        sink = jnp.concatenate(
            [jnp.broadcast_to(sink_ref[:, h * GROUP + g:h * GROUP + g + 1], (w, 1))
             for g in range(GROUP)], axis=0)
        sc = _dot_nt(q4, kb) * (HEAD_DIM ** -0.5)
        sc = jnp.where(valid, sc, -jnp.inf)
        m = jnp.maximum(jnp.max(sc, axis=-1, keepdims=True), sink)
        e = jnp.exp(sc - m)
        probs = e / (jnp.sum(e, axis=-1, keepdims=True) + jnp.exp(sink - m))
        out = _dot(probs, vb)
        for g in range(GROUP):
            c0 = (h * GROUP + g) * HEAD_DIM
            o_ref[:, c0:c0 + HEAD_DIM] = out[g * w:(g + 1) * w]


def swa_sink_attention(q, kv, sinks, seq):
    t, dq = q.shape
    dkv = kv.shape[1] // 2
    nb = t // WINDOW
    cur = lambda i: (i, 0)
    prev = lambda i: (jnp.maximum(i - 1, 0), 0)
    cur_v = lambda i: (i, 1)
    prev_v = lambda i: (jnp.maximum(i - 1, 0), 1)
    return pl.pallas_call(
        functools.partial(_swa_kernel, blocks_per_seq=seq // WINDOW, n_kv=dkv // HEAD_DIM),
        out_shape=jax.ShapeDtypeStruct((t, dq), F32),
        grid=(nb,),
        in_specs=[pl.BlockSpec((WINDOW, dq), cur),
                  pl.BlockSpec((WINDOW, dkv), prev), pl.BlockSpec((WINDOW, dkv), cur),
                  pl.BlockSpec((WINDOW, dkv), prev_v), pl.BlockSpec((WINDOW, dkv), cur_v),
                  pl.BlockSpec((1, sinks.shape[1]), lambda i: (0, 0))],
        out_specs=pl.BlockSpec((WINDOW, dq), cur),
        compiler_params=_params(1), name="swa_sink_attention",
    )(q, kv, kv, kv, kv, sinks)


def _rope(t, cos, sin_signed):
    n = t.shape[1]
    reps = n // LANES
    cos = jnp.concatenate([cos] * reps, axis=1) if reps > 1 else cos
    sin_signed = jnp.concatenate([sin_signed] * reps, axis=1) if reps > 1 else sin_signed
    lane = lax.broadcasted_iota(jnp.int32, t.shape, 1)
    half = HEAD_DIM // 2
    partner = jnp.where(lane % HEAD_DIM < half, pltpu.roll(t, n - half, axis=1),
                        pltpu.roll(t, half, axis=1))
    return t * cos + partner * sin_signed


def _rope_tables(positions):
    inv_freq = ROPE_THETA ** (-jnp.arange(0, HEAD_DIM, 2, dtype=F32) / HEAD_DIM)
    ang = positions.astype(F32).reshape(-1, 1) * inv_freq
    cos, sin = jnp.cos(ang), jnp.sin(ang)
    reps = LANES // HEAD_DIM
    return (jnp.concatenate([cos, cos] * reps, axis=1),
            jnp.concatenate([-sin, sin] * reps, axis=1))


def kernel(x, p, positions, norm_g, ffn_w_gate, ffn_w_up, ffn_w_down, ple_w_up, ple_w_gate, rwkv_mu, rwkv_w_rkv, rwkv_w_o, rwkv_w0, rwkv_w1, rwkv_w2, rwkv_a0, rwkv_a1, rwkv_a2, rwkv_v0, rwkv_v1, rwkv_v2, rwkv_g1, rwkv_g2, rwkv_k_k, rwkv_k_a, rwkv_r_k, rwkv_gn_g, rwkv_gn_b, kv_norm_g, w_kv, attn_w_q, attn_w_o, attn_sinks, final_norm_g):
    bsz, seq, d = x.shape
    depth = norm_g.shape[0]
    n_a = rwkv_mu.shape[0]
    t = bsz * seq
    x = x.reshape(t, d)
    cos, sin_signed = _rope_tables(positions)
    row = lambda v: v.reshape(1, -1)
    ident = lambda z: z
    tm = 256
    tok_tile = lambda width: ((tm, width), lambda i, j: (i, j))

    v_first = None
    kv = None
    for i in range(depth):
        if i == n_a:
            n_kv_half = w_kv.shape[1] // 2
            kv = fused_matmul(
                [x, row(kv_norm_g)], w_kv.astype(BF16),
                [(cos, (tm, LANES), lambda i_, j: (i_, 0)),
                 (sin_signed, (tm, LANES), lambda i_, j: (i_, 0))],
                _rms,
                lambda acc, c, s: jnp.where(pl.program_id(1) == 0, _rope(acc, c, s), acc),
                1, w_kv.shape[1], tm=tm, tn=n_kv_half, name="kv_proj")

        x = ffn_half_step(x, row(norm_g[i, 0]), ffn_w_gate[i, 0].astype(BF16),
                          ffn_w_up[i, 0].astype(BF16), ffn_w_down[i, 0].astype(BF16))

        if i < n_a:
            j = i
            h, dx = norm_shift(x, row(norm_g[i, 1]), seq)
            mu = rwkv_mu[j]
            mix = lambda c: (lambda h_, dx_, mu_: h_ + dx_ * mu_)
            r, k, v = [
                fused_matmul([h, dx, row(mu[c])], rwkv_w_rkv[j, n].astype(BF16), [], mix(c),
                             ident, 1, d, tm=tm, tn=d, name="rwkv_proj_" + "rkv"[n])
                for n, c in ((0, 0), (1, 2), (2, 3))]
            lw = lora(h, dx, row(mu[1]), rwkv_w1[j], rwkv_w2[j], row(rwkv_w0[j]), jnp.tanh,
                      lambda z: -DECAY_SCALE * _sigmoid(z), name="rwkv_lora_w")
            a = lora(h, dx, row(mu[4]), rwkv_a1[j], rwkv_a2[j], row(rwkv_a0[j]), ident,
                     _sigmoid, name="rwkv_lora_a")
            g = lora(h, dx, row(mu[5]), rwkv_g1[j], rwkv_g2[j], jnp.zeros((1, d), F32), _sigmoid,
                     ident, name="rwkv_lora_g")
            if j == 0:
                vmix = None
                v_first = v
            else:
                vg = lora(h, dx, row(mu[3]), rwkv_v1[j - 1], rwkv_v2[j - 1], row(rwkv_v0[j - 1]),
                          ident, _sigmoid, name="rwkv_lora_v")
                vmix = (v_first, vg)
            y = wkv7(r, k, v, lw, a, vmix, row(rwkv_k_k[j]), row(rwkv_k_a[j]),
                     row(rwkv_r_k[j]), row(rwkv_gn_g[j]), row(rwkv_gn_b[j]), seq)
            x = fused_matmul([y, g], rwkv_w_o[j].astype(BF16), [(x,) + tok_tile(d)],
                             lambda y_, g_: y_ * g_, lambda acc, res: res + acc,
                             1, d, tm=tm, tn=d, name="rwkv_out_proj")
        else:
            j = i - n_a
            q = fused_matmul(
                [x, row(norm_g[i, 1])], attn_w_q[j].astype(BF16),
                [(cos, (tm, LANES), lambda i_, j_: (i_, 0)),
                 (sin_signed, (tm, LANES), lambda i_, j_: (i_, 0))],
                _rms, _rope, 1, d, tm=tm, tn=d, name="attn_q_proj")
            o = swa_sink_attention(q, kv, row(attn_sinks[j]), seq)
            x = fused_matmul([o], attn_w_o[j].astype(BF16), [(x,) + tok_tile(d)],
                             ident, lambda acc, res: res + acc,
                             1, d, tm=tm, tn=d, name="attn_out_proj")

        x = ffn_half_step(x, row(norm_g[i, 2]), ffn_w_gate[i, 1].astype(BF16),
                          ffn_w_up[i, 1].astype(BF16), ffn_w_down[i, 1].astype(BF16))

        last = i == depth - 1
        fin = row(final_norm_g)

        def ple_epilogue(acc, res, p_tile, w_up, fin_g, last=last):
            out = res + _sigmoid(acc) * _dot(p_tile.astype(BF16), w_up)
            return _rms(out, fin_g) if last else out

        x = fused_matmul(
            [x, row(norm_g[i, 3])], ple_w_gate[i].astype(BF16),
            [(x,) + tok_tile(d),
             (p[i].reshape(t, -1), (tm, p.shape[-1]), lambda i_, j_: (i_, 0)),
             (ple_w_up[i].astype(BF16), (p.shape[-1], d), lambda i_, j_: (0, 0)),
             (fin, (1, d), lambda i_, j_: (0, 0))],
            _rms, ple_epilogue, 1, d, tm=tm, tn=d, name="ple")
    return x.reshape(bsz, seq, d)
```

```python
import functools
import math

import jax
import jax.numpy as jnp
from jax import lax
from jax.experimental import pallas as pl
from jax.experimental.pallas import tpu as pltpu

F32 = jnp.float32
BF16 = jnp.bfloat16

HEAD_DIM = 64
LANES = 128
CHUNK = 64
WINDOW = 128
GROUP = 4
ROPE_THETA = 10000.0
RMS_EPS = 1e-6
GN_EPS = 64e-5
DECAY_SCALE = math.exp(-0.5)
VMEM_LIMIT = 56 * 1024 * 1024


def _params(n_axes):
    return pltpu.CompilerParams(dimension_semantics=("arbitrary",) * n_axes,
                                vmem_limit_bytes=VMEM_LIMIT)


def _rms(x, g):
    return x * lax.rsqrt(jnp.mean(x * x, axis=-1, keepdims=True) + RMS_EPS) * g


def _sigmoid(x):
    return 1.0 / (1.0 + jnp.exp(-x))


def _dot(a, b):
    return jnp.dot(a, b, preferred_element_type=F32)


def _dot_nt(a, b):
    return lax.dot_general(a, b, (((1,), (1,)), ((), ())), preferred_element_type=F32)


def _dot_tn(a, b):
    return lax.dot_general(a, b, (((0,), (0,)), ((), ())), preferred_element_type=F32)


def _fmm_kernel(*refs, n_row, n_extra, n_out, prologue, epilogue):
    row = refs[:n_row]
    w_ref = refs[n_row]
    extra = refs[n_row + 1:n_row + 1 + n_extra]
    outs = refs[n_row + 1 + n_extra:n_row + 1 + n_extra + n_out]
    lhs_ref = refs[-1]

    @pl.when(pl.program_id(1) == 0)
    def _():
        lhs_ref[...] = prologue(*[r[...] for r in row]).astype(BF16)

    acc = _dot(lhs_ref[...], w_ref[...])
    res = epilogue(acc, *[e[...] for e in extra])
    if n_out == 1:
        res = (res,)
    for o, v in zip(outs, res):
        o[...] = v


def fused_matmul(rows, w, extras, prologue, epilogue, n_out, out_width, *, tm, tn, name):
    t = max(r.shape[0] for r in rows)
    k, n = w.shape
    grid = (t // tm, n // tn)
    in_specs = []
    for r in rows:
        if r.shape[0] == 1:
            in_specs.append(pl.BlockSpec((1, r.shape[1]), lambda i, j: (0, 0)))
        else:
            in_specs.append(pl.BlockSpec((tm, r.shape[1]), lambda i, j: (i, 0)))
    if tn == n:
        in_specs.append(pl.BlockSpec((k, tn), lambda i, j: (0, 0), pipeline_mode=pl.Buffered(1)))
    else:
        in_specs.append(pl.BlockSpec((k, tn), lambda i, j: (0, j)))
    for arr, bs, im in extras:
        in_specs.append(pl.BlockSpec(bs, im))
    otn = out_width // (n // tn)
    out_shape = [jax.ShapeDtypeStruct((t, out_width), F32)] * n_out
    out_specs = [pl.BlockSpec((tm, otn), lambda i, j: (i, j))] * n_out
    res = pl.pallas_call(
        functools.partial(_fmm_kernel, n_row=len(rows), n_extra=len(extras), n_out=n_out,
                          prologue=prologue, epilogue=epilogue),
        out_shape=out_shape, grid=grid, in_specs=in_specs, out_specs=out_specs,
        scratch_shapes=[pltpu.VMEM((tm, k), BF16)],
        compiler_params=_params(2), name=name,
    )(*rows, w, *[e[0] for e in extras])
    return res[0] if n_out == 1 else res


def _ffn_kernel(x_ref, g_ref, wg_ref, wu_ref, wd_ref, o_ref, h_ref):
    f = pl.program_id(1)

    @pl.when(f == 0)
    def _():
        x = x_ref[...]
        h_ref[...] = _rms(x, g_ref[...]).astype(BF16)
        o_ref[...] = x

    h = h_ref[...]
    gate = _dot(h, wg_ref[...])
    up = _dot(h, wu_ref[...])
    act = (gate * _sigmoid(gate) * up).astype(BF16)
    o_ref[...] += _dot(act, wd_ref[...])


def ffn_half_step(x, g, wg, wu, wd_half, *, tm=512, tf=512):
    t, d = x.shape
    f = wg.shape[1]
    return pl.pallas_call(
        _ffn_kernel,
        out_shape=jax.ShapeDtypeStruct((t, d), F32),
        grid=(t // tm, f // tf),
        in_specs=[pl.BlockSpec((tm, d), lambda i, j: (i, 0)),
                  pl.BlockSpec((1, d), lambda i, j: (0, 0)),
                  pl.BlockSpec((d, tf), lambda i, j: (0, j)),
                  pl.BlockSpec((d, tf), lambda i, j: (0, j)),
                  pl.BlockSpec((tf, d), lambda i, j: (j, 0))],
        out_specs=pl.BlockSpec((tm, d), lambda i, j: (i, 0)),
        scratch_shapes=[pltpu.VMEM((tm, d), BF16)],
        compiler_params=_params(2), name="ffn_half_step",
    )(x, g, wg, wu, wd_half)


def _norm_shift_kernel(x_ref, g_ref, h_ref, dx_ref, carry_ref, *, tiles_per_seq):
    i = pl.program_id(0)

    @pl.when(i == 0)
    def _():
        carry_ref[...] = jnp.zeros_like(carry_ref)

    h = _rms(x_ref[...], g_ref[...])
    tm = h.shape[0]
    last = carry_ref[7:8, :]
    first = jnp.where(i % tiles_per_seq == 0, jnp.zeros_like(last), last)
    rolled = pltpu.roll(h, 1, axis=0)
    row = lax.broadcasted_iota(jnp.int32, h.shape, 0)
    h_prev = jnp.where(row == 0, first, rolled)
    h_ref[...] = h
    dx_ref[...] = h_prev - h
    carry_ref[...] = h[tm - 8:, :]


def norm_shift(x, g, seq, *, tm=512):
    t, d = x.shape
    return pl.pallas_call(
        functools.partial(_norm_shift_kernel, tiles_per_seq=seq // tm),
        out_shape=[jax.ShapeDtypeStruct((t, d), F32)] * 2,
        grid=(t // tm,),
        in_specs=[pl.BlockSpec((tm, d), lambda i: (i, 0)),
                  pl.BlockSpec((1, d), lambda i: (0, 0))],
        out_specs=[pl.BlockSpec((tm, d), lambda i: (i, 0))] * 2,
        scratch_shapes=[pltpu.VMEM((8, d), F32)],
        compiler_params=_params(1), name="norm_shift",
    )(x, g)


def _lora_kernel(h_ref, dx_ref, mu_ref, w1_ref, w2_ref, b_ref, o_ref, *, mid_act, out_act):
    xm = (h_ref[...] + dx_ref[...] * mu_ref[...]).astype(BF16)
    mid = mid_act(_dot(xm, w1_ref[...]))
    z = _dot(mid.astype(BF16), w2_ref[...]) + b_ref[...]
    o_ref[...] = out_act(z)


def lora(h, dx, mu, w1, w2, bias, mid_act, out_act, *, tm=512, name):
    t, d = h.shape
    r = w1.shape[1]
    rp = -(-r // LANES) * LANES
    w1 = jnp.pad(w1, ((0, 0), (0, rp - r))).astype(BF16)
    w2 = jnp.pad(w2, ((0, rp - r), (0, 0))).astype(BF16)
    n = w2.shape[1]
    return pl.pallas_call(
        functools.partial(_lora_kernel, mid_act=mid_act, out_act=out_act),
        out_shape=jax.ShapeDtypeStruct((t, n), F32),
        grid=(t // tm,),
        in_specs=[pl.BlockSpec((tm, d), lambda i: (i, 0)),
                  pl.BlockSpec((tm, d), lambda i: (i, 0)),
                  pl.BlockSpec((1, d), lambda i: (0, 0)),
                  pl.BlockSpec((d, rp), lambda i: (0, 0)),
                  pl.BlockSpec((rp, n), lambda i: (0, 0)),
                  pl.BlockSpec((1, n), lambda i: (0, 0))],
        out_specs=pl.BlockSpec((tm, n), lambda i: (i, 0)),
        compiler_params=_params(1), name=name,
    )(h, dx, mu, w1, w2, bias)


def _wkv_kernel(*refs, n_pairs, n_chunks, has_vmix):
    if has_vmix:
        (r_ref, k_ref, v_ref, lw_ref, a_ref, vf_ref, vg_ref,
         kk_ref, ka_ref, rk_ref, gg_ref, gb_ref, y_ref, s_ref) = refs
    else:
        (r_ref, k_ref, v_ref, lw_ref, a_ref,
         kk_ref, ka_ref, rk_ref, gg_ref, gb_ref, y_ref, s_ref) = refs

    @pl.when(pl.program_id(2) == 0)
    def _():
        s_ref[...] = jnp.zeros_like(s_ref)

    L = CHUNK
    items = [(c, p) for c in range(n_chunks) for p in range(n_pairs)]
    row = lax.broadcasted_iota(jnp.int32, (L, LANES), 0)
    col = lax.broadcasted_iota(jnp.int32, (L, LANES), 1)
    strict = (col % L) < row
    incl = (col % L) <= row
    eye2 = ((col % L) == row).astype(F32)
    tri = (lax.broadcasted_iota(jnp.int32, (L, L), 1)
           <= lax.broadcasted_iota(jnp.int32, (L, L), 0)).astype(BF16)
    tri2 = jnp.concatenate([tri, tri], axis=1)
    brow = lax.broadcasted_iota(jnp.int32, (LANES, LANES), 0)
    bcol = lax.broadcasted_iota(jnp.int32, (LANES, LANES), 1)
    same_head = (brow // HEAD_DIM) == (bcol // HEAD_DIM)
    ones_bd = same_head.astype(F32)
    head0 = col < HEAD_DIM
    zeros_bd = jnp.zeros((LANES, LANES), F32)

    def bd(x):
        return jnp.where(same_head, jnp.concatenate([x, x], axis=0), 0.0)

    def tile(ref, c, p):
        return ref[c * L:(c + 1) * L, p * LANES:(p + 1) * LANES]

    def vec(ref, p):
        return ref[:, p * LANES:(p + 1) * LANES]

    def stacked_dot(xs, w):
        out = _dot(jnp.concatenate(xs, axis=0), w)
        return [out[i * L:(i + 1) * L] for i in range(len(xs))]

    lanes2 = lambda x, y: jnp.concatenate([x, y], axis=1)
    rows2 = lambda x, y: jnp.concatenate([x, y], axis=0)

    r = [tile(r_ref, c, p) for c, p in items]
    k = [tile(k_ref, c, p) for c, p in items]
    v = [tile(v_ref, c, p) for c, p in items]
    lw = [tile(lw_ref, c, p) for c, p in items]
    a = [tile(a_ref, c, p) for c, p in items]
    if has_vmix:
        v = [vi + (tile(vf_ref, c, p) - vi) * tile(vg_ref, c, p) for vi, (c, p) in zip(v, items)]
    kx = [ki * vec(kk_ref, p) for ki, (c, p) in zip(k, items)]
    ss = stacked_dot([x * x for x in kx], ones_bd)
    kk = [x * lax.rsqrt(jnp.maximum(s, 1e-24)) for x, s in zip(kx, ss)]
    kh = [ki * (1.0 + (ai - 1.0) * vec(ka_ref, p)) for ki, ai, (c, p) in zip(k, a, items)]
    b_s = [x * ai for x, ai in zip(kk, a)]

    lw_hi = [x.astype(BF16) for x in lw]
    lw_lo = [(x - h.astype(F32)).astype(BF16) for x, h in zip(lw, lw_hi)]
    cum = [_dot(tri2, rows2(h, l)) for h, l in zip(lw_hi, lw_lo)]
    cum_last = [x[L - 1:L, :] for x in cum]
    e_fwd = [jnp.exp(x) for x in cum]
    e_inv = [jnp.exp(-x) for x in cum]
    e_tail = [jnp.exp(xl - x) for x, xl in zip(cum, cum_last)]
    r_t = [x * e for x, e in zip(r, e_fwd)]
    a_t = [-x * jnp.exp(c_ - l_) for x, c_, l_ in zip(kk, cum, lw)]
    b_t = [x * e for x, e in zip(b_s, e_inv)]
    k_t = [x * e for x, e in zip(kh, e_inv)]
    b_hat = [x * e for x, e in zip(b_s, e_tail)]
    k_hat = [x * e for x, e in zip(kh, e_tail)]

    a_all = [_dot_nt(rows2(at, rt),
                     jnp.concatenate([jnp.where(head0, bt, 0.0), jnp.where(head0, 0.0, bt),
                                      jnp.where(head0, kt, 0.0), jnp.where(head0, 0.0, kt)], axis=0))
             for at, rt, bt, kt in zip(a_t, r_t, b_t, k_t)]
    a_ab = [jnp.where(strict, x[:L, :LANES], 0.0) for x in a_all]
    a_ak = [jnp.where(strict, x[:L, LANES:], 0.0) for x in a_all]
    a_rb = [jnp.where(incl, x[L:, :LANES], 0.0) for x in a_all]
    a_rk = [jnp.where(incl, x[L:, LANES:], 0.0) for x in a_all]

    v_bd = [bd(x) for x in v]
    akv = [_dot(x, w) for x, w in zip(a_ak, v_bd)]
    pw = [_dot(x, bd(x)) for x in a_ab]
    t_inv = [eye2 + x for x in a_ab]
    for _ in range(int(math.log2(L)) - 2):
        res = [_dot(p_, lanes2(bd(t_), bd(p_))) for p_, t_ in zip(pw, t_inv)]
        t_inv = [t_ + x[:, :LANES] for t_, x in zip(t_inv, res)]
        pw = [x[:, LANES:] for x in res]
    t_inv = [t_ + _dot(p_, bd(t_)) for p_, t_ in zip(pw, t_inv)]
    au = [_dot(t_, lanes2(bd(at), bd(x))) for t_, at, x in zip(t_inv, a_t, akv)]
    a2 = [x[:, :LANES] for x in au]
    u0 = [x[:, LANES:] for x in au]
    qy = [_dot(lanes2(rb, rk_),
               rows2(lanes2(bd(a2_), bd(u0_)), lanes2(zeros_bd, vb)))
          for rb, rk_, a2_, u0_, vb in zip(a_rb, a_rk, a2, u0, v_bd)]
    q = [rt + x[:, :LANES] for rt, x in zip(r_t, qy)]
    y0 = [x[:, LANES:] for x in qy]
    rkr = stacked_dot([ri * khi * vec(rk_ref, p) for ri, khi, (c, p) in zip(r, kh, items)], ones_bd)
    bonus = [x * vi for x, vi in zip(rkr, v)]

    s = [s_ref[p] for p in range(n_pairs)]
    y = []
    for c in range(n_chunks):
        idx = [c * n_pairs + p for p in range(n_pairs)]
        fs = [_dot_nt(rows2(a2[i], q[i]), s[p]) for p, i in enumerate(idx)]
        u = [u0[i] + f[:L] for f, i in zip(fs, idx)]
        y += [y0[i] + f[L:] for f, i in zip(fs, idx)]
        upd = [_dot_tn(rows2(ui, v[i]), rows2(b_hat[i], k_hat[i])) for ui, i in zip(u, idx)]
        s = [s[p] * jnp.exp(cum_last[i]) + jnp.where(same_head, upd[p], 0.0)
             for p, i in enumerate(idx)]
    for p in range(n_pairs):
        s_ref[p] = s[p]

    mean = stacked_dot(y, ones_bd)
    dev = [yi - m * (1.0 / HEAD_DIM) for yi, m in zip(y, mean)]
    var = stacked_dot([x * x for x in dev], ones_bd)
    for i, (c, p) in enumerate(items):
        yn = dev[i] * lax.rsqrt(var[i] * (1.0 / HEAD_DIM) + GN_EPS) * vec(gg_ref, p) + vec(gb_ref, p)
        y_ref[c * L:(c + 1) * L, p * LANES:(p + 1) * LANES] = yn + bonus[i]


def wkv7(r, k, v, lw, a, vmix, k_k, k_a, r_k, gn_g, gn_b, seq, *, n_pairs=4, n_chunks=4):
    t, d = r.shape
    gw = n_pairs * LANES
    rows = n_chunks * CHUNK
    grid = (t // seq, d // gw, seq // rows)
    steps = seq // rows
    tok = pl.BlockSpec((rows, gw), lambda b, g, c: (b * steps + c, g))
    vec = pl.BlockSpec((1, gw), lambda b, g, c: (0, g))
    toks = [r, k, v, lw, a] + (list(vmix) if vmix is not None else [])
    vecs = [k_k, k_a, r_k, gn_g, gn_b]
    return pl.pallas_call(
        functools.partial(_wkv_kernel, n_pairs=n_pairs, n_chunks=n_chunks,
                          has_vmix=vmix is not None),
        out_shape=jax.ShapeDtypeStruct((t, d), F32),
        grid=grid,
        in_specs=[tok] * len(toks) + [vec] * len(vecs),
        out_specs=tok,
        scratch_shapes=[pltpu.VMEM((n_pairs, LANES, LANES), F32)],
        compiler_params=_params(3), name="wkv7",
    )(*toks, *vecs)


def _swa_kernel(q_ref, kp_ref, kc_ref, vp_ref, vc_ref, sink_ref, o_ref, *, blocks_per_seq, n_kv):
    n = pl.program_id(0) % blocks_per_seq
    w = WINDOW
    qi = lax.broadcasted_iota(jnp.int32, (w, 2 * w), 0)
    ki = lax.broadcasted_iota(jnp.int32, (w, 2 * w), 1)
    rel = w + qi - ki
    valid = (rel >= 0) & (rel < w) & ((ki >= w) | (n > 0))
    lane = lax.broadcasted_iota(jnp.int32, (2 * w, LANES), 1)
    low = lane < HEAD_DIM
    low_o = lax.broadcasted_iota(jnp.int32, (w, LANES), 1) < HEAD_DIM
    pairs = range(n_kv // 2)
    tiles = [(g, half) for g in range(GROUP) for half in range(2)]

    def band(p_ref, c_ref, j):
        cols = slice(j * LANES, (j + 1) * LANES)
        b = jnp.concatenate([p_ref[:, cols], c_ref[:, cols]], axis=0)
        return jnp.concatenate([jnp.where(low, b, 0.0), jnp.where(low, 0.0, b)], axis=0)

    k2 = [band(kp_ref, kc_ref, j) for j in pairs]
    v2 = [band(vp_ref, vc_ref, j) for j in pairs]
    q4 = [jnp.concatenate([q_ref[:, (j * GROUP + g) * LANES:(j * GROUP + g + 1) * LANES]
                           for g in range(GROUP)], axis=0) for j in pairs]
    sc = [_dot_nt(q, k) * (HEAD_DIM ** -0.5) for q, k in zip(q4, k2)]
    sink = [[sink_ref[(2 * j + half) * GROUP + g] for g, half in tiles] for j in pairs]
    s = [[jnp.where(valid, x[g * w:(g + 1) * w, half * 2 * w:(half + 1) * 2 * w], -jnp.inf)
          for g, half in tiles] for x in sc]
    m = [[jnp.maximum(jnp.max(x, axis=-1, keepdims=True), sk) for x, sk in zip(xs, sks)]
         for xs, sks in zip(s, sink)]
    e = [[jnp.exp(x - mx) for x, mx in zip(xs, ms)] for xs, ms in zip(s, m)]
    inv = [[1.0 / (jnp.sum(x, axis=-1, keepdims=True) + jnp.exp(sk - mx))
            for x, sk, mx in zip(xs, sks, ms)] for xs, sks, ms in zip(e, sink, m)]
    e4 = [jnp.concatenate([jnp.concatenate([xs[2 * g], xs[2 * g + 1]], axis=1)
                           for g in range(GROUP)], axis=0) for xs in e]
    out = [_dot(x, v) for x, v in zip(e4, v2)]
    for j in pairs:
        for g in range(GROUP):
            scale = jnp.where(low_o, inv[j][2 * g], inv[j][2 * g + 1])
            o_ref[:, (j * GROUP + g) * LANES:(j * GROUP + g + 1) * LANES] = (
                out[j][g * w:(g + 1) * w] * scale)


def paired_head_order(n_kv):
    return [(2 * j + half) * GROUP + g
            for j in range(n_kv // 2) for g in range(GROUP) for half in range(2)]


def swa_sink_attention(q, kv, sinks, seq):
    t, dq = q.shape
    dkv = kv.shape[1] // 2
    nb = t // WINDOW
    cur = lambda i: (i, 0)
    prev = lambda i: (jnp.maximum(i - 1, 0), 0)
    cur_v = lambda i: (i, 1)
    prev_v = lambda i: (jnp.maximum(i - 1, 0), 1)
    return pl.pallas_call(
        functools.partial(_swa_kernel, blocks_per_seq=seq // WINDOW, n_kv=dkv // HEAD_DIM),
        out_shape=jax.ShapeDtypeStruct((t, dq), F32),
        grid=(nb,),
        in_specs=[pl.BlockSpec((WINDOW, dq), cur),
                  pl.BlockSpec((WINDOW, dkv), prev), pl.BlockSpec((WINDOW, dkv), cur),
                  pl.BlockSpec((WINDOW, dkv), prev_v), pl.BlockSpec((WINDOW, dkv), cur_v),
                  pl.BlockSpec(memory_space=pltpu.SMEM)],
        out_specs=pl.BlockSpec((WINDOW, dq), cur),
        compiler_params=_params(1), name="swa_sink_attention",
    )(q, kv, kv, kv, kv, sinks)


def _rope(t, cos, sin_signed):
    n = t.shape[1]
    reps = n // LANES
    cos = jnp.concatenate([cos] * reps, axis=1) if reps > 1 else cos
    sin_signed = jnp.concatenate([sin_signed] * reps, axis=1) if reps > 1 else sin_signed
    lane = lax.broadcasted_iota(jnp.int32, t.shape, 1)
    half = HEAD_DIM // 2
    partner = jnp.where(lane % HEAD_DIM < half, pltpu.roll(t, n - half, axis=1),
                        pltpu.roll(t, half, axis=1))
    return t * cos + partner * sin_signed


def _rope_tables(positions):
    inv_freq = ROPE_THETA ** (-jnp.arange(0, HEAD_DIM, 2, dtype=F32) / HEAD_DIM)
    ang = positions.astype(F32).reshape(-1, 1) * inv_freq
    cos, sin = jnp.cos(ang), jnp.sin(ang)
    reps = LANES // HEAD_DIM
    return (jnp.concatenate([cos, cos] * reps, axis=1),
            jnp.concatenate([-sin, sin] * reps, axis=1))


def kernel(x, p, positions, norm_g, ffn_w_gate, ffn_w_up, ffn_w_down, ple_w_up, ple_w_gate, rwkv_mu, rwkv_w_rkv, rwkv_w_o, rwkv_w0, rwkv_w1, rwkv_w2, rwkv_a0, rwkv_a1, rwkv_a2, rwkv_v0, rwkv_v1, rwkv_v2, rwkv_g1, rwkv_g2, rwkv_k_k, rwkv_k_a, rwkv_r_k, rwkv_gn_g, rwkv_gn_b, kv_norm_g, w_kv, attn_w_q, attn_w_o, attn_sinks, final_norm_g):
    bsz, seq, d = x.shape
    depth = norm_g.shape[0]
    n_a = rwkv_mu.shape[0]
    t = bsz * seq
    x = x.reshape(t, d)
    cos, sin_signed = _rope_tables(positions)
    row = lambda v: v.reshape(1, -1)
    ident = lambda z: z
    tm = 256
    tok_tile = lambda width: ((tm, width), lambda i, j: (i, j))

    v_first = None
    kv = None
    for i in range(depth):
        if i == n_a:
            n_kv_half = w_kv.shape[1] // 2
            kv = fused_matmul(
                [x, row(kv_norm_g)], w_kv.astype(BF16),
                [(cos, (tm, LANES), lambda i_, j: (i_, 0)),
                 (sin_signed, (tm, LANES), lambda i_, j: (i_, 0))],
                _rms,
                lambda acc, c, s: jnp.where(pl.program_id(1) == 0, _rope(acc, c, s), acc),
                1, w_kv.shape[1], tm=tm, tn=n_kv_half, name="kv_proj")

        x = ffn_half_step(x, row(norm_g[i, 0]), ffn_w_gate[i, 0].astype(BF16),
                          ffn_w_up[i, 0].astype(BF16), (0.5 * ffn_w_down[i, 0]).astype(BF16))

        if i < n_a:
            j = i
            h, dx = norm_shift(x, row(norm_g[i, 1]), seq)
            mu = rwkv_mu[j]
            mix = lambda c: (lambda h_, dx_, mu_: h_ + dx_ * mu_)
            r, k, v = [
                fused_matmul([h, dx, row(mu[c])], rwkv_w_rkv[j, n].astype(BF16), [], mix(c),
                             ident, 1, d, tm=tm, tn=d, name="rwkv_proj_" + "rkv"[n])
                for n, c in ((0, 0), (1, 2), (2, 3))]
            lw = lora(h, dx, row(mu[1]), rwkv_w1[j], rwkv_w2[j], row(rwkv_w0[j]), jnp.tanh,
                      lambda z: -DECAY_SCALE * _sigmoid(z), name="rwkv_lora_w")
            a = lora(h, dx, row(mu[4]), rwkv_a1[j], rwkv_a2[j], row(rwkv_a0[j]), ident,
                     _sigmoid, name="rwkv_lora_a")
            g = lora(h, dx, row(mu[5]), rwkv_g1[j], rwkv_g2[j], jnp.zeros((1, d), F32), _sigmoid,
                     ident, name="rwkv_lora_g")
            if j == 0:
                vmix = None
                v_first = v
            else:
                vg = lora(h, dx, row(mu[3]), rwkv_v1[j - 1], rwkv_v2[j - 1], row(rwkv_v0[j - 1]),
                          ident, _sigmoid, name="rwkv_lora_v")
                vmix = (v_first, vg)
            y = wkv7(r, k, v, lw, a, vmix, row(rwkv_k_k[j]), row(rwkv_k_a[j]),
                     row(rwkv_r_k[j]), row(rwkv_gn_g[j]), row(rwkv_gn_b[j]), seq)
            x = fused_matmul([y, g], rwkv_w_o[j].astype(BF16), [(x,) + tok_tile(d)],
                             lambda y_, g_: y_ * g_, lambda acc, res: res + acc,
                             1, d, tm=tm, tn=d, name="rwkv_out_proj")
        else:
            j = i - n_a
            n_q = attn_sinks.shape[1]
            order = jnp.asarray(paired_head_order(n_q // GROUP))
            w_q = attn_w_q[j].reshape(d, n_q, HEAD_DIM)[:, order].reshape(d, n_q * HEAD_DIM)
            w_o = attn_w_o[j].reshape(n_q, HEAD_DIM, d)[order].reshape(n_q * HEAD_DIM, d)
            q = fused_matmul(
                [x, row(norm_g[i, 1])], w_q.astype(BF16),
                [(cos, (tm, LANES), lambda i_, j_: (i_, 0)),
                 (sin_signed, (tm, LANES), lambda i_, j_: (i_, 0))],
                _rms, _rope, 1, d, tm=tm, tn=d, name="attn_q_proj")
            o = swa_sink_attention(q, kv, attn_sinks[j], seq)
            x = fused_matmul([o], w_o.astype(BF16), [(x,) + tok_tile(d)],
                             ident, lambda acc, res: res + acc,
                             1, d, tm=tm, tn=d, name="attn_out_proj")

        x = ffn_half_step(x, row(norm_g[i, 2]), ffn_w_gate[i, 1].astype(BF16),
                          ffn_w_up[i, 1].astype(BF16), (0.5 * ffn_w_down[i, 1]).astype(BF16))

        last = i == depth - 1
        fin = row(final_norm_g)

        def ple_epilogue(acc, res, p_tile, w_up, fin_g, last=last):
            out = res + _sigmoid(acc) * _dot(p_tile.astype(BF16), w_up)
            return _rms(out, fin_g) if last else out

        x = fused_matmul(
            [x, row(norm_g[i, 3])], ple_w_gate[i].astype(BF16),
            [(x,) + tok_tile(d),
             (p[i].reshape(t, -1), (tm, p.shape[-1]), lambda i_, j_: (i_, 0)),
             (ple_w_up[i].astype(BF16), (p.shape[-1], d), lambda i_, j_: (0, 0)),
             (fin, (1, d), lambda i_, j_: (0, 0))],
            _rms, ple_epilogue, 1, d, tm=tm, tn=d, name="ple")
    return x.reshape(bsz, seq, d)
```

```python
import functools
import math

import jax
import jax.numpy as jnp
from jax import lax
from jax.experimental import pallas as pl
from jax.experimental.pallas import tpu as pltpu

F32 = jnp.float32
BF16 = jnp.bfloat16

HEAD_DIM = 64
LANES = 128
CHUNK = 64
WINDOW = 128
GROUP = 4
ROPE_THETA = 10000.0
RMS_EPS = 1e-6
GN_EPS = 64e-5
DECAY_SCALE = math.exp(-0.5)
VMEM_LIMIT = 56 * 1024 * 1024


def _params(n_axes, **kw):
    return pltpu.CompilerParams(dimension_semantics=("arbitrary",) * n_axes,
                                vmem_limit_bytes=VMEM_LIMIT, **kw)


def _rms(x, g):
    return x * lax.rsqrt(jnp.mean(x * x, axis=-1, keepdims=True) + RMS_EPS) * g


def _sigmoid(x):
    return 1.0 / (1.0 + jnp.exp(-x))


def _dot(a, b):
    return jnp.dot(a.astype(BF16), b.astype(BF16), preferred_element_type=F32)


def _dot_nt(a, b):
    return lax.dot_general(a.astype(BF16), b.astype(BF16), (((1,), (1,)), ((), ())),
                           preferred_element_type=F32)


def _dot_tn(a, b):
    return lax.dot_general(a.astype(BF16), b.astype(BF16), (((0,), (0,)), ((), ())),
                           preferred_element_type=F32)


def _fmm_kernel(*refs, n_row, n_extra, n_out, prologue, epilogue):
    row = refs[:n_row]
    w_ref = refs[n_row]
    extra = refs[n_row + 1:n_row + 1 + n_extra]
    outs = refs[n_row + 1 + n_extra:n_row + 1 + n_extra + n_out]
    lhs_ref = refs[-1]

    @pl.when(pl.program_id(1) == 0)
    def _():
        lhs_ref[...] = prologue(*[r[...] for r in row]).astype(BF16)

    acc = _dot(lhs_ref[...], w_ref[...])
    res = epilogue(acc, *[e[...] for e in extra])
    if n_out == 1:
        res = (res,)
    for o, v in zip(outs, res):
        o[...] = v


def fused_matmul(rows, w, extras, prologue, epilogue, n_out, out_width, *, tm, tn, name, lead=()):
    t = max(r.shape[0] for r in rows)
    k, n = w.shape[-2:]
    grid = (t // tm, n // tn)
    in_specs = []
    for r in rows:
        if r.shape[0] == 1:
            in_specs.append(pl.BlockSpec((1, r.shape[1]), lambda i, j: (0, 0)))
        else:
            in_specs.append(pl.BlockSpec((tm, r.shape[1]), lambda i, j: (i, 0)))
    w_block = (None,) * len(lead) + (k, tn)
    if tn == n:
        in_specs.append(pl.BlockSpec(w_block, lambda i, j: lead + (0, 0),
                                     pipeline_mode=pl.Buffered(1)))
    else:
        in_specs.append(pl.BlockSpec(w_block, lambda i, j: lead + (0, j)))
    for arr, bs, im in extras:
        in_specs.append(pl.BlockSpec(bs, im))
    otn = out_width // (n // tn)
    out_shape = [jax.ShapeDtypeStruct((t, out_width), F32)] * n_out
    out_specs = [pl.BlockSpec((tm, otn), lambda i, j: (i, j))] * n_out
    res = pl.pallas_call(
        functools.partial(_fmm_kernel, n_row=len(rows), n_extra=len(extras), n_out=n_out,
                          prologue=prologue, epilogue=epilogue),
        out_shape=out_shape, grid=grid, in_specs=in_specs, out_specs=out_specs,
        scratch_shapes=[pltpu.VMEM((tm, k), BF16)],
        compiler_params=_params(2), name=name,
    )(*rows, w, *[e[0] for e in extras])
    return res[0] if n_out == 1 else res


def _ffn_kernel(x_ref, g_ref, wg_ref, wu_ref, wd_ref, o_ref, h_ref):
    f = pl.program_id(1)

    @pl.when(f == 0)
    def _():
        x = x_ref[...]
        h_ref[...] = _rms(x, g_ref[...]).astype(BF16)
        o_ref[...] = x

    h = h_ref[...]
    gate = _dot(h, wg_ref[...])
    up = _dot(h, wu_ref[...])
    act = (gate * _sigmoid(gate) * up).astype(BF16)
    o_ref[...] += _dot(act, wd_ref[...])


def ffn_half_step(x, g, wg, wu, wd_half, lead, *, tm=512, tf=512):
    t, d = x.shape
    f = wg.shape[-1]
    none = (None,) * len(lead)
    return pl.pallas_call(
        _ffn_kernel,
        out_shape=jax.ShapeDtypeStruct((t, d), F32),
        grid=(t // tm, f // tf),
        in_specs=[pl.BlockSpec((tm, d), lambda i, j: (i, 0)),
                  pl.BlockSpec((1, d), lambda i, j: (0, 0)),
                  pl.BlockSpec(none + (d, tf), lambda i, j: lead + (0, j)),
                  pl.BlockSpec(none + (d, tf), lambda i, j: lead + (0, j)),
                  pl.BlockSpec(none + (tf, d), lambda i, j: lead + (j, 0))],
        out_specs=pl.BlockSpec((tm, d), lambda i, j: (i, 0)),
        scratch_shapes=[pltpu.VMEM((tm, d), BF16)],
        compiler_params=_params(2), name="ffn_half_step",
    )(x, g, wg, wu, wd_half)


def _rwkv_in_kernel(*refs, tiles_per_seq, n_col, has_vgate):
    x_ref, g_ref, mu_ref, w_ref = refs[:4]
    n_lora = 11 if has_vgate else 8
    lora_refs = refs[4:4 + n_lora]
    outs = refs[4 + n_lora:-2]
    xm_ref, carry_ref = refs[-2:]
    rkv_ref, lw_ref, a_ref, gate_ref = outs[:4]
    i = pl.program_id(0)
    j = pl.program_id(1)

    @pl.when((i == 0) & (j == 0))
    def _():
        carry_ref[...] = jnp.zeros_like(carry_ref)

    @pl.when(j == 0)
    def _():
        h = _rms(x_ref[...], g_ref[...])
        tm = h.shape[0]
        last = carry_ref[7:8, :]
        first = jnp.where(i % tiles_per_seq == 0, jnp.zeros_like(last), last)
        row = lax.broadcasted_iota(jnp.int32, h.shape, 0)
        dx = jnp.where(row == 0, first, pltpu.roll(h, 1, axis=0)) - h
        carry_ref[...] = h[tm - 8:, :]

        def mix(c):
            return (h + dx * mu_ref[c:c + 1, :]).astype(BF16)

        def low_rank(xm, w1_ref, w2_ref, mid_act):
            return _dot(mid_act(_dot(xm, w1_ref[...])).astype(BF16), w2_ref[...])

        xm_ref[0] = mix(0)
        xm_ref[1] = mix(2)
        xv = mix(3)
        xm_ref[2] = xv
        w1w, w2w, b_w, w1a, w2a, b_a, w1g, w2g = lora_refs[:8]
        lw_ref[...] = -DECAY_SCALE * _sigmoid(low_rank(mix(1), w1w, w2w, jnp.tanh) + b_w[...])
        a_ref[...] = _sigmoid(low_rank(mix(4), w1a, w2a, lambda z: z) + b_a[...])
        gate_ref[...] = low_rank(mix(5), w1g, w2g, _sigmoid)
        if has_vgate:
            w1v, w2v, b_v = lora_refs[8:]
            outs[4][...] = _sigmoid(low_rank(xv, w1v, w2v, lambda z: z) + b_v[...])

    rkv_ref[...] = _dot(xm_ref[j // n_col], w_ref[...])


def _pad_rank(w1, w2):
    r = w1.shape[1]
    rp = -(-r // LANES) * LANES
    return (jnp.pad(w1, ((0, 0), (0, rp - r))).astype(BF16),
            jnp.pad(w2, ((0, rp - r), (0, 0))).astype(BF16))


def rwkv_in(x, g, mu, w_rkv, layer, loras, seq, *, tm=256, tn=512):
    t, d = x.shape
    n_col = d // tn
    has_vgate = len(loras) == 4
    const = lambda shape: pl.BlockSpec(shape, lambda i, j: (0,) * len(shape),
                                       pipeline_mode=pl.Buffered(1))
    lora_args, lora_specs = [], []
    for w1, w2, bias in loras:
        w1, w2 = _pad_rank(w1, w2)
        lora_args += [w1, w2] + ([bias.reshape(1, d)] if bias is not None else [])
        lora_specs += [const(w1.shape), const(w2.shape)] + ([const((1, d))] if bias is not None else [])
    n_out = 5 if has_vgate else 4
    tok = pl.BlockSpec((tm, d), lambda i, j: (i, 0))
    return pl.pallas_call(
        functools.partial(_rwkv_in_kernel, tiles_per_seq=seq // tm, n_col=n_col,
                          has_vgate=has_vgate),
        out_shape=[jax.ShapeDtypeStruct((t, 3 * d), F32)] + [jax.ShapeDtypeStruct((t, d), F32)] * (n_out - 1),
        grid=(t // tm, 3 * n_col),
        in_specs=[tok, const((1, d)), const((6, d)),
                  pl.BlockSpec((None, None, d, tn), lambda i, j: (layer, j // n_col, 0, j % n_col))]
                 + lora_specs,
        out_specs=[pl.BlockSpec((tm, tn), lambda i, j: (i, j))] + [tok] * (n_out - 1),
        scratch_shapes=[pltpu.VMEM((3, tm, d), BF16), pltpu.VMEM((8, d), F32)],
        compiler_params=_params(2), name="rwkv_in",
    )(x, g.reshape(1, d), mu, w_rkv, *lora_args)


def _wkv_kernel(*refs, n_pairs, n_chunks, has_vmix):
    if has_vmix:
        (r_ref, k_ref, v_ref, lw_ref, a_ref, vf_ref, vg_ref,
         kk_ref, ka_ref, rk_ref, gg_ref, gb_ref, y_ref, s_ref) = refs
    else:
        (r_ref, k_ref, v_ref, lw_ref, a_ref,
         kk_ref, ka_ref, rk_ref, gg_ref, gb_ref, y_ref, s_ref) = refs

    @pl.when(pl.program_id(2) == 0)
    def _():
        s_ref[...] = jnp.zeros_like(s_ref)

    L = CHUNK
    items = [(c, p) for c in range(n_chunks) for p in range(n_pairs)]
    row = lax.broadcasted_iota(jnp.int32, (L, LANES), 0)
    col = lax.broadcasted_iota(jnp.int32, (L, LANES), 1)
    strict = (col % L) < row
    incl = (col % L) <= row
    eye2 = ((col % L) == row).astype(F32)
    tri = (lax.broadcasted_iota(jnp.int32, (L, L), 1)
           <= lax.broadcasted_iota(jnp.int32, (L, L), 0)).astype(BF16)
    tri2 = jnp.concatenate([tri, tri], axis=1)
    brow = lax.broadcasted_iota(jnp.int32, (LANES, LANES), 0)
    bcol = lax.broadcasted_iota(jnp.int32, (LANES, LANES), 1)
    same_head = (brow // HEAD_DIM) == (bcol // HEAD_DIM)
    ones_bd = same_head.astype(F32)
    head0 = col < HEAD_DIM
    zeros_bd = jnp.zeros((LANES, LANES), BF16)

    def bd(x):
        xb = x.astype(BF16)
        return jnp.where(same_head, jnp.concatenate([xb, xb], axis=0), jnp.zeros((), BF16))

    def tile(ref, c, p):
        return ref[c * L:(c + 1) * L, p * LANES:(p + 1) * LANES]

    def vec(ref, p):
        return ref[:, p * LANES:(p + 1) * LANES]

    def stacked_dot(xs, w):
        out = _dot(jnp.concatenate(xs, axis=0), w)
        return [out[i * L:(i + 1) * L] for i in range(len(xs))]

    lanes2 = lambda x, y: jnp.concatenate([x, y], axis=1)
    rows2 = lambda x, y: jnp.concatenate([x, y], axis=0)

    r = [tile(r_ref, c, p) for c, p in items]
    k = [tile(k_ref, c, p) for c, p in items]
    v = [tile(v_ref, c, p) for c, p in items]
    lw = [tile(lw_ref, c, p) for c, p in items]
    a = [tile(a_ref, c, p) for c, p in items]
    if has_vmix:
        v = [vi + (tile(vf_ref, c, p) - vi) * tile(vg_ref, c, p) for vi, (c, p) in zip(v, items)]
    kx = [ki * vec(kk_ref, p) for ki, (c, p) in zip(k, items)]
    ss = stacked_dot([x * x for x in kx], ones_bd)
    kk = [x * lax.rsqrt(jnp.maximum(s, 1e-24)) for x, s in zip(kx, ss)]
    kh = [ki * (1.0 + (ai - 1.0) * vec(ka_ref, p)) for ki, ai, (c, p) in zip(k, a, items)]
    b_s = [x * ai for x, ai in zip(kk, a)]

    lw_hi = [x.astype(BF16) for x in lw]
    lw_lo = [(x - h.astype(F32)).astype(BF16) for x, h in zip(lw, lw_hi)]
    cum = [_dot(tri2, rows2(h, l)) for h, l in zip(lw_hi, lw_lo)]
    cum_last = [x[L - 1:L, :] for x in cum]
    e_fwd = [jnp.exp(x) for x in cum]
    e_inv = [jnp.exp(-x) for x in cum]
    e_tail = [jnp.exp(xl - x) for x, xl in zip(cum, cum_last)]
    r_t = [x * e for x, e in zip(r, e_fwd)]
    a_t = [-x * jnp.exp(c_ - l_) for x, c_, l_ in zip(kk, cum, lw)]
    b_t = [x * e for x, e in zip(b_s, e_inv)]
    k_t = [x * e for x, e in zip(kh, e_inv)]
    b_hat = [x * e for x, e in zip(b_s, e_tail)]
    k_hat = [x * e for x, e in zip(kh, e_tail)]

    a_all = [_dot_nt(rows2(at, rt),
                     jnp.concatenate([jnp.where(head0, bt, 0.0), jnp.where(head0, 0.0, bt),
                                      jnp.where(head0, kt, 0.0), jnp.where(head0, 0.0, kt)], axis=0))
             for at, rt, bt, kt in zip(a_t, r_t, b_t, k_t)]
    a_ab = [jnp.where(strict, x[:L, :LANES], 0.0) for x in a_all]
    a_ak = [jnp.where(strict, x[:L, LANES:], 0.0) for x in a_all]
    a_rb = [jnp.where(incl, x[L:, :LANES], 0.0) for x in a_all]
    a_rk = [jnp.where(incl, x[L:, LANES:], 0.0) for x in a_all]

    v_bd = [bd(x) for x in v]
    akv = [_dot(x, w) for x, w in zip(a_ak, v_bd)]
    pw = [_dot(x, bd(x)) for x in a_ab]
    t_inv = [eye2 + x for x in a_ab]
    for _ in range(int(math.log2(L)) - 2):
        res = [_dot(p_, lanes2(bd(t_), bd(p_))) for p_, t_ in zip(pw, t_inv)]
        t_inv = [t_ + x[:, :LANES] for t_, x in zip(t_inv, res)]
        pw = [x[:, LANES:] for x in res]
    t_inv = [t_ + _dot(p_, bd(t_)) for p_, t_ in zip(pw, t_inv)]
    au = [_dot(t_, lanes2(bd(at), bd(x))) for t_, at, x in zip(t_inv, a_t, akv)]
    a2 = [x[:, :LANES] for x in au]
    u0 = [x[:, LANES:] for x in au]
    qy = [_dot(lanes2(rb, rk_),
               rows2(lanes2(bd(a2_), bd(u0_)), lanes2(zeros_bd, vb)))
          for rb, rk_, a2_, u0_, vb in zip(a_rb, a_rk, a2, u0, v_bd)]
    q = [rt + x[:, :LANES] for rt, x in zip(r_t, qy)]
    y0 = [x[:, LANES:] for x in qy]
    rkr = stacked_dot([ri * khi * vec(rk_ref, p) for ri, khi, (c, p) in zip(r, kh, items)], ones_bd)
    bonus = [x * vi for x, vi in zip(rkr, v)]

    s = [s_ref[p] for p in range(n_pairs)]
    y = []
    for c in range(n_chunks):
        idx = [c * n_pairs + p for p in range(n_pairs)]
        fs = [_dot_nt(rows2(a2[i], q[i]), s[p]) for p, i in enumerate(idx)]
        u = [u0[i] + f[:L] for f, i in zip(fs, idx)]
        y += [y0[i] + f[L:] for f, i in zip(fs, idx)]
        upd = [_dot_tn(rows2(ui, v[i]), rows2(b_hat[i], k_hat[i])) for ui, i in zip(u, idx)]
        s = [s[p] * jnp.exp(cum_last[i]) + jnp.where(same_head, upd[p], 0.0)
             for p, i in enumerate(idx)]
    for p in range(n_pairs):
        s_ref[p] = s[p]

    mean = stacked_dot(y, ones_bd)
    dev = [yi - m * (1.0 / HEAD_DIM) for yi, m in zip(y, mean)]
    var = stacked_dot([x * x for x in dev], ones_bd)
    for i, (c, p) in enumerate(items):
        yn = dev[i] * lax.rsqrt(var[i] * (1.0 / HEAD_DIM) + GN_EPS) * vec(gg_ref, p) + vec(gb_ref, p)
        y_ref[c * L:(c + 1) * L, p * LANES:(p + 1) * LANES] = yn + bonus[i]


def wkv7(rkv, lw, a, vmix, k_k, k_a, r_k, gn_g, gn_b, seq, *, n_chunks=1):
    t, d = lw.shape
    n_pairs = d // LANES
    gw = n_pairs * LANES
    rows = n_chunks * CHUNK
    n_grp = d // gw
    grid = (t // seq, n_grp, seq // rows)
    steps = seq // rows
    tok_at = lambda part: pl.BlockSpec((rows, gw), lambda b, g, c: (b * steps + c, part * n_grp + g))
    tok = tok_at(0)
    vec = pl.BlockSpec((1, gw), lambda b, g, c: (0, g))
    toks = [rkv, rkv, rkv, lw, a] + (list(vmix) if vmix is not None else [])
    tok_specs = [tok_at(0), tok_at(1), tok_at(2), tok, tok] + ([tok_at(2), tok] if vmix is not None else [])
    vecs = [k_k, k_a, r_k, gn_g, gn_b]
    return pl.pallas_call(
        functools.partial(_wkv_kernel, n_pairs=n_pairs, n_chunks=n_chunks,
                          has_vmix=vmix is not None),
        out_shape=jax.ShapeDtypeStruct((t, d), F32),
        grid=grid,
        in_specs=tok_specs + [vec] * len(vecs),
        out_specs=tok,
        scratch_shapes=[pltpu.VMEM((n_pairs, LANES, LANES), F32)],
        compiler_params=_params(3), name="wkv7",
    )(*toks, *vecs)


def _swa_kernel(q_ref, kp_ref, kc_ref, vp_ref, vc_ref, sink_ref, o_ref, *, blocks_per_seq, n_kv):
    n = pl.program_id(0) % blocks_per_seq
    w = WINDOW
    qi = lax.broadcasted_iota(jnp.int32, (w, 2 * w), 0)
    ki = lax.broadcasted_iota(jnp.int32, (w, 2 * w), 1)
    rel = w + qi - ki
    valid = (rel >= 0) & (rel < w) & ((ki >= w) | (n > 0))
    lane = lax.broadcasted_iota(jnp.int32, (2 * w, LANES), 1)
    low = lane < HEAD_DIM
    low_o = lax.broadcasted_iota(jnp.int32, (w, LANES), 1) < HEAD_DIM
    pairs = range(n_kv // 2)
    tiles = [(g, half) for g in range(GROUP) for half in range(2)]

    def band(p_ref, c_ref, j):
        cols = slice(j * LANES, (j + 1) * LANES)
        b = jnp.concatenate([p_ref[:, cols], c_ref[:, cols]], axis=0).astype(BF16)
        zero = jnp.zeros((), BF16)
        return jnp.concatenate([jnp.where(low, b, zero), jnp.where(low, zero, b)], axis=0)

    k2 = [band(kp_ref, kc_ref, j) for j in pairs]
    v2 = [band(vp_ref, vc_ref, j) for j in pairs]
    q4 = [jnp.concatenate([q_ref[:, (j * GROUP + g) * LANES:(j * GROUP + g + 1) * LANES]
                           for g in range(GROUP)], axis=0) for j in pairs]
    sc = [_dot_nt(q, k) * (HEAD_DIM ** -0.5) for q, k in zip(q4, k2)]
    sink = [[sink_ref[(2 * j + half) * GROUP + g] for g, half in tiles] for j in pairs]
    s = [[jnp.where(valid, x[g * w:(g + 1) * w, half * 2 * w:(half + 1) * 2 * w], -jnp.inf)
          for g, half in tiles] for x in sc]
    m = [[jnp.maximum(jnp.max(x, axis=-1, keepdims=True), sk) for x, sk in zip(xs, sks)]
         for xs, sks in zip(s, sink)]
    e = [[jnp.exp(x - mx) for x, mx in zip(xs, ms)] for xs, ms in zip(s, m)]
    inv = [[1.0 / (jnp.sum(x, axis=-1, keepdims=True) + jnp.exp(sk - mx))
            for x, sk, mx in zip(xs, sks, ms)] for xs, sks, ms in zip(e, sink, m)]
    e4 = [jnp.concatenate([jnp.concatenate([xs[2 * g], xs[2 * g + 1]], axis=1)
                           for g in range(GROUP)], axis=0) for xs in e]
    out = [_dot(x, v) for x, v in zip(e4, v2)]
    for j in pairs:
        for g in range(GROUP):
            scale = jnp.where(low_o, inv[j][2 * g], inv[j][2 * g + 1])
            o_ref[:, (j * GROUP + g) * LANES:(j * GROUP + g + 1) * LANES] = (
                out[j][g * w:(g + 1) * w] * scale)


def paired_head_order(n_kv):
    return [(2 * j + half) * GROUP + g
            for j in range(n_kv // 2) for g in range(GROUP) for half in range(2)]


def swa_sink_attention(q, kv, sinks, seq):
    t, dq = q.shape
    dkv = kv.shape[1] // 2
    nb = t // WINDOW
    cur = lambda i: (i, 0)
    prev = lambda i: (jnp.maximum(i - 1, 0), 0)
    cur_v = lambda i: (i, 1)
    prev_v = lambda i: (jnp.maximum(i - 1, 0), 1)
    return pl.pallas_call(
        functools.partial(_swa_kernel, blocks_per_seq=seq // WINDOW, n_kv=dkv // HEAD_DIM),
        out_shape=jax.ShapeDtypeStruct((t, dq), F32),
        grid=(nb,),
        in_specs=[pl.BlockSpec((WINDOW, dq), cur),
                  pl.BlockSpec((WINDOW, dkv), prev), pl.BlockSpec((WINDOW, dkv), cur),
                  pl.BlockSpec((WINDOW, dkv), prev_v), pl.BlockSpec((WINDOW, dkv), cur_v),
                  pl.BlockSpec(memory_space=pltpu.SMEM)],
        out_specs=pl.BlockSpec((WINDOW, dq), cur),
        compiler_params=_params(1), name="swa_sink_attention",
    )(q, kv, kv, kv, kv, sinks)


def _rope(t, cos, sin_signed):
    n = t.shape[1]
    reps = n // LANES
    cos = jnp.concatenate([cos] * reps, axis=1) if reps > 1 else cos
    sin_signed = jnp.concatenate([sin_signed] * reps, axis=1) if reps > 1 else sin_signed
    lane = lax.broadcasted_iota(jnp.int32, t.shape, 1)
    half = HEAD_DIM // 2
    partner = jnp.where(lane % HEAD_DIM < half, pltpu.roll(t, n - half, axis=1),
                        pltpu.roll(t, half, axis=1))
    return t * cos + partner * sin_signed


def _rope_tables(positions):
    inv_freq = ROPE_THETA ** (-jnp.arange(0, HEAD_DIM, 2, dtype=F32) / HEAD_DIM)
    ang = positions.astype(F32).reshape(-1, 1) * inv_freq
    cos, sin = jnp.cos(ang), jnp.sin(ang)
    reps = LANES // HEAD_DIM
    return (jnp.concatenate([cos, cos] * reps, axis=1),
            jnp.concatenate([-sin, sin] * reps, axis=1))


def kernel(x, p, positions, norm_g, ffn_w_gate, ffn_w_up, ffn_w_down, ple_w_up, ple_w_gate, rwkv_mu, rwkv_w_rkv, rwkv_w_o, rwkv_w0, rwkv_w1, rwkv_w2, rwkv_a0, rwkv_a1, rwkv_a2, rwkv_v0, rwkv_v1, rwkv_v2, rwkv_g1, rwkv_g2, rwkv_k_k, rwkv_k_a, rwkv_r_k, rwkv_gn_g, rwkv_gn_b, kv_norm_g, w_kv, attn_w_q, attn_w_o, attn_sinks, final_norm_g):
    bsz, seq, d = x.shape
    depth = norm_g.shape[0]
    n_a = rwkv_mu.shape[0]
    t = bsz * seq
    x = x.reshape(t, d)
    cos, sin_signed = _rope_tables(positions)
    row = lambda v: v.reshape(1, -1)
    ident = lambda z: z
    tm = 512
    tok_tile = lambda width: ((tm, width), lambda i, j: (i, j))

    wg_all = ffn_w_gate.astype(BF16)
    wu_all = ffn_w_up.astype(BF16)
    wd_all = (0.5 * ffn_w_down).astype(BF16)
    w_rkv_all = rwkv_w_rkv.astype(BF16)
    w_ro_all = rwkv_w_o.astype(BF16)
    w_pg_all = ple_w_gate.astype(BF16)
    n_q = attn_sinks.shape[1]
    order = jnp.asarray(paired_head_order(n_q // GROUP))
    n_b = attn_w_q.shape[0]
    w_q_all = attn_w_q.reshape(n_b, d, n_q, HEAD_DIM)[:, :, order].reshape(n_b, d, -1).astype(BF16)
    w_o_all = attn_w_o.reshape(n_b, n_q, HEAD_DIM, d)[:, order].reshape(n_b, -1, d).astype(BF16)

    rkv_first = None
    kv = None
    for i in range(depth):
        if i == n_a:
            n_kv_half = w_kv.shape[1] // 2
            kv = fused_matmul(
                [x, row(kv_norm_g)], w_kv.astype(BF16),
                [(cos, (tm, LANES), lambda i_, j: (i_, 0)),
                 (sin_signed, (tm, LANES), lambda i_, j: (i_, 0))],
                _rms,
                lambda acc, c, s: jnp.where(pl.program_id(1) == 0, _rope(acc, c, s), acc),
                1, w_kv.shape[1], tm=tm, tn=n_kv_half, name="kv_proj")

        x = ffn_half_step(x, row(norm_g[i, 0]), wg_all, wu_all, wd_all, (i, 0))

        if i < n_a:
            j = i
            loras = [(rwkv_w1[j], rwkv_w2[j], rwkv_w0[j]), (rwkv_a1[j], rwkv_a2[j], rwkv_a0[j]),
                     (rwkv_g1[j], rwkv_g2[j], None)]
            if j > 0:
                loras.append((rwkv_v1[j - 1], rwkv_v2[j - 1], rwkv_v0[j - 1]))
            rkv, lw, a, g, *vg = rwkv_in(x, norm_g[i, 1], rwkv_mu[j], w_rkv_all, j, loras, seq)
            if j == 0:
                rkv_first = rkv
            vmix = (rkv_first, vg[0]) if vg else None
            y = wkv7(rkv, lw, a, vmix, row(rwkv_k_k[j]), row(rwkv_k_a[j]),
                     row(rwkv_r_k[j]), row(rwkv_gn_g[j]), row(rwkv_gn_b[j]), seq)
            x = fused_matmul([y, g], w_ro_all, [(x,) + tok_tile(d)],
                             lambda y_, g_: y_ * g_, lambda acc, res: res + acc,
                             1, d, tm=tm, tn=d, name="rwkv_out_proj", lead=(j,))
        else:
            j = i - n_a
            q = fused_matmul(
                [x, row(norm_g[i, 1])], w_q_all,
                [(cos, (tm, LANES), lambda i_, j_: (i_, 0)),
                 (sin_signed, (tm, LANES), lambda i_, j_: (i_, 0))],
                _rms, _rope, 1, d, tm=tm, tn=d, name="attn_q_proj", lead=(j,))
            o = swa_sink_attention(q, kv, attn_sinks[j], seq)
            x = fused_matmul([o], w_o_all, [(x,) + tok_tile(d)],
                             ident, lambda acc, res: res + acc,
                             1, d, tm=tm, tn=d, name="attn_out_proj", lead=(j,))

        x = ffn_half_step(x, row(norm_g[i, 2]), wg_all, wu_all, wd_all, (i, 1))

        last = i == depth - 1
        fin = row(final_norm_g)

        def ple_epilogue(acc, res, p_tile, w_up, fin_g, last=last):
            out = res + _sigmoid(acc) * _dot(p_tile.astype(BF16), w_up)
            return _rms(out, fin_g) if last else out

        x = fused_matmul(
            [x, row(norm_g[i, 3])], w_pg_all,
            [(x,) + tok_tile(d),
             (p[i].reshape(t, -1), (tm, p.shape[-1]), lambda i_, j_: (i_, 0)),
             (ple_w_up[i].astype(BF16), (p.shape[-1], d), lambda i_, j_: (0, 0)),
             (fin, (1, d), lambda i_, j_: (0, 0))],
            _rms, ple_epilogue, 1, d, tm=tm, tn=d, name="ple", lead=(i,))
    return x.reshape(bsz, seq, d)
```

```python
import functools
import math

import jax
import jax.numpy as jnp
from jax import lax
from jax.experimental import pallas as pl
from jax.experimental.pallas import tpu as pltpu

F32 = jnp.float32
BF16 = jnp.bfloat16

HEAD_DIM = 64
LANES = 128
CHUNK = 64
WINDOW = 128
GROUP = 4
ROPE_THETA = 10000.0
RMS_EPS = 1e-6
GN_EPS = 64e-5
DECAY_SCALE = math.exp(-0.5)
VMEM_LIMIT = 56 * 1024 * 1024


def _params(n_axes, **kw):
    return pltpu.CompilerParams(dimension_semantics=("arbitrary",) * n_axes,
                                vmem_limit_bytes=VMEM_LIMIT, **kw)


def _rms(x, g):
    return x * lax.rsqrt(jnp.mean(x * x, axis=-1, keepdims=True) + RMS_EPS) * g


def _sigmoid(x):
    return 1.0 / (1.0 + jnp.exp(-x))


def _dot(a, b):
    return jnp.dot(a.astype(BF16), b.astype(BF16), preferred_element_type=F32)


def _dot_nt(a, b):
    return lax.dot_general(a.astype(BF16), b.astype(BF16), (((1,), (1,)), ((), ())),
                           preferred_element_type=F32)


def _dot_tn(a, b):
    return lax.dot_general(a.astype(BF16), b.astype(BF16), (((0,), (0,)), ((), ())),
                           preferred_element_type=F32)


def _fmm_kernel(*refs, n_row, n_extra, n_out, prologue, epilogue):
    row = refs[:n_row]
    w_ref = refs[n_row]
    extra = refs[n_row + 1:n_row + 1 + n_extra]
    outs = refs[n_row + 1 + n_extra:n_row + 1 + n_extra + n_out]
    lhs_ref = refs[-1]

    @pl.when(pl.program_id(1) == 0)
    def _():
        lhs_ref[...] = prologue(*[r[...] for r in row]).astype(BF16)

    acc = _dot(lhs_ref[...], w_ref[...])
    res = epilogue(acc, *[e[...] for e in extra])
    if n_out == 1:
        res = (res,)
    for o, v in zip(outs, res):
        o[...] = v


def fused_matmul(rows, w, extras, prologue, epilogue, n_out, out_width, *, tm, tn, name, lead=()):
    k, n = w.shape[-2:]
    in_specs = []
    row_args = []
    for r in rows:
        if isinstance(r, tuple):
            arr, part = r
            in_specs.append(pl.BlockSpec((tm, k), lambda i, j, part=part: (i, part)))
        elif r.shape[0] == 1:
            arr = r
            in_specs.append(pl.BlockSpec((1, r.shape[1]), lambda i, j: (0, 0)))
        else:
            arr = r
            in_specs.append(pl.BlockSpec((tm, r.shape[1]), lambda i, j: (i, 0)))
        row_args.append(arr)
    rows = row_args
    t = max(r.shape[0] for r in rows)
    grid = (t // tm, n // tn)
    w_block = (None,) * len(lead) + (k, tn)
    if tn == n:
        in_specs.append(pl.BlockSpec(w_block, lambda i, j: lead + (0, 0),
                                     pipeline_mode=pl.Buffered(1)))
    else:
        in_specs.append(pl.BlockSpec(w_block, lambda i, j: lead + (0, j)))
    for arr, bs, im in extras:
        in_specs.append(pl.BlockSpec(bs, im))
    otn = out_width // (n // tn)
    out_shape = [jax.ShapeDtypeStruct((t, out_width), F32)] * n_out
    out_specs = [pl.BlockSpec((tm, otn), lambda i, j: (i, j))] * n_out
    res = pl.pallas_call(
        functools.partial(_fmm_kernel, n_row=len(rows), n_extra=len(extras), n_out=n_out,
                          prologue=prologue, epilogue=epilogue),
        out_shape=out_shape, grid=grid, in_specs=in_specs, out_specs=out_specs,
        scratch_shapes=[pltpu.VMEM((tm, k), BF16)],
        compiler_params=_params(2), name=name,
    )(*rows, w, *[e[0] for e in extras])
    return res[0] if n_out == 1 else res


def _ffn_kernel(x_ref, g_ref, wg_ref, wu_ref, wd_ref, o_ref, h_ref):
    f = pl.program_id(1)

    @pl.when(f == 0)
    def _():
        x = x_ref[...]
        h_ref[...] = _rms(x, g_ref[...]).astype(BF16)
        o_ref[...] = x

    h = h_ref[...]
    gate = _dot(h, wg_ref[...])
    up = _dot(h, wu_ref[...])
    act = (gate * _sigmoid(gate) * up).astype(BF16)
    o_ref[...] += _dot(act, wd_ref[...])


def ffn_half_step(x, g, wg, wu, wd_half, lead, *, tm=512, tf=512):
    t, d = x.shape
    f = wg.shape[-1]
    none = (None,) * len(lead)
    return pl.pallas_call(
        _ffn_kernel,
        out_shape=jax.ShapeDtypeStruct((t, d), F32),
        grid=(t // tm, f // tf),
        in_specs=[pl.BlockSpec((tm, d), lambda i, j: (i, 0)),
                  pl.BlockSpec((1, d), lambda i, j: (0, 0)),
                  pl.BlockSpec(none + (d, tf), lambda i, j: lead + (0, j)),
                  pl.BlockSpec(none + (d, tf), lambda i, j: lead + (0, j)),
                  pl.BlockSpec(none + (tf, d), lambda i, j: lead + (j, 0))],
        out_specs=pl.BlockSpec((tm, d), lambda i, j: (i, 0)),
        scratch_shapes=[pltpu.VMEM((tm, d), BF16)],
        compiler_params=_params(2), name="ffn_half_step",
    )(x, g, wg, wu, wd_half)


def _rwkv_in_kernel(*refs, tiles_per_seq, n_col, has_vgate):
    n_br = 4 if has_vgate else 3
    x_ref, g_ref, mu_ref, w_ref = refs[:4]
    w1 = refs[4:4 + n_br]
    w2 = refs[4 + n_br:4 + 2 * n_br]
    bias = refs[4 + 2 * n_br:3 + 3 * n_br]
    outs = refs[3 + 3 * n_br:3 + 3 * n_br + (3 if has_vgate else 2)]
    xm_ref, carry_ref = refs[-2 - n_br:-n_br]
    mid = refs[-n_br:]
    rkv_ref, lag_ref = outs[:2]
    i = pl.program_id(0)
    j = pl.program_id(1)

    @pl.when((i == 0) & (j == 0))
    def _():
        carry_ref[...] = jnp.zeros_like(carry_ref)

    @pl.when(j == 0)
    def _():
        h = _rms(x_ref[...], g_ref[...])
        tm = h.shape[0]
        last = carry_ref[7:8, :]
        first = jnp.where(i % tiles_per_seq == 0, jnp.zeros_like(last), last)
        row = lax.broadcasted_iota(jnp.int32, h.shape, 0)
        dx = jnp.where(row == 0, first, pltpu.roll(h, 1, axis=0)) - h
        carry_ref[...] = h[tm - 8:, :]

        def mix(c):
            return (h + dx * mu_ref[c:c + 1, :]).astype(BF16)

        xm_ref[0] = mix(0)
        xm_ref[1] = mix(2)
        xv = mix(3)
        xm_ref[2] = xv
        mid[0][...] = jnp.tanh(_dot(mix(1), w1[0][...])).astype(BF16)
        mid[1][...] = _dot(mix(4), w1[1][...]).astype(BF16)
        mid[2][...] = _sigmoid(_dot(mix(5), w1[2][...])).astype(BF16)
        if has_vgate:
            mid[3][...] = _dot(xv, w1[3][...]).astype(BF16)

    c = j // n_col
    rkv_ref[...] = _dot(xm_ref[c], w_ref[...])

    @pl.when(c == 0)
    def _():
        lag_ref[...] = -DECAY_SCALE * _sigmoid(_dot(mid[0][...], w2[0][...]) + bias[0][...])
        if has_vgate:
            outs[2][...] = _sigmoid(_dot(mid[3][...], w2[3][...]) + bias[2][...])

    @pl.when(c == 1)
    def _():
        lag_ref[...] = _sigmoid(_dot(mid[1][...], w2[1][...]) + bias[1][...])

    @pl.when(c == 2)
    def _():
        lag_ref[...] = _dot(mid[2][...], w2[2][...])


def _pad_rank(w1, w2):
    r = w1.shape[1]
    rp = -(-r // LANES) * LANES
    return (jnp.pad(w1, ((0, 0), (0, rp - r))).astype(BF16),
            jnp.pad(w2, ((0, rp - r), (0, 0))).astype(BF16))


def rwkv_in(x, g, mu, w_rkv, layer, loras, seq, *, tm=512, tn=512):
    t, d = x.shape
    n_col = d // tn
    has_vgate = len(loras) == 4
    const = lambda shape: pl.BlockSpec(shape, lambda i, j: (0,) * len(shape),
                                       pipeline_mode=pl.Buffered(1))
    col = lambda rows: pl.BlockSpec((rows, tn), lambda i, j: (0, j % n_col))
    padded = [_pad_rank(w1, w2) for w1, w2, _ in loras]
    w1s = [w1 for w1, _ in padded]
    w2s = [w2 for _, w2 in padded]
    biases = [b.reshape(1, d) for _, _, b in loras if b is not None]
    out_shape = [jax.ShapeDtypeStruct((t, 3 * d), F32)] * 2
    out_specs = [pl.BlockSpec((tm, tn), lambda i, j: (i, j))] * 2
    if has_vgate:
        out_shape.append(jax.ShapeDtypeStruct((t, d), F32))
        out_specs.append(pl.BlockSpec((tm, tn), lambda i, j: (i, jnp.minimum(j, n_col - 1))))
    return pl.pallas_call(
        functools.partial(_rwkv_in_kernel, tiles_per_seq=seq // tm, n_col=n_col,
                          has_vgate=has_vgate),
        out_shape=out_shape,
        grid=(t // tm, 3 * n_col),
        in_specs=[pl.BlockSpec((tm, d), lambda i, j: (i, 0)), const((1, d)), const((6, d)),
                  pl.BlockSpec((None, None, d, tn), lambda i, j: (layer, j // n_col, 0, j % n_col))]
                 + [const(w.shape) for w in w1s] + [col(w.shape[0]) for w in w2s]
                 + [col(1) for _ in biases],
        out_specs=out_specs,
        scratch_shapes=[pltpu.VMEM((3, tm, d), BF16), pltpu.VMEM((8, d), F32)]
                       + [pltpu.VMEM((tm, w.shape[1]), BF16) for w in w1s],
        compiler_params=_params(2), name="rwkv_in",
    )(x, g.reshape(1, d), mu, w_rkv, *w1s, *w2s, *biases)


def _wkv_kernel(*refs, n_pairs, n_chunks, has_vmix):
    if has_vmix:
        (r_ref, k_ref, v_ref, lw_ref, a_ref, vf_ref, vg_ref,
         kk_ref, ka_ref, rk_ref, gg_ref, gb_ref, y_ref, s_ref) = refs
    else:
        (r_ref, k_ref, v_ref, lw_ref, a_ref,
         kk_ref, ka_ref, rk_ref, gg_ref, gb_ref, y_ref, s_ref) = refs

    @pl.when(pl.program_id(2) == 0)
    def _():
        s_ref[...] = jnp.zeros_like(s_ref)

    L = CHUNK
    items = [(c, p) for c in range(n_chunks) for p in range(n_pairs)]
    row = lax.broadcasted_iota(jnp.int32, (L, LANES), 0)
    col = lax.broadcasted_iota(jnp.int32, (L, LANES), 1)
    strict = (col % L) < row
    incl = (col % L) <= row
    eye2 = ((col % L) == row).astype(F32)
    tri = (lax.broadcasted_iota(jnp.int32, (L, L), 1)
           <= lax.broadcasted_iota(jnp.int32, (L, L), 0)).astype(BF16)
    tri2 = jnp.concatenate([tri, tri], axis=1)
    brow = lax.broadcasted_iota(jnp.int32, (LANES, LANES), 0)
    bcol = lax.broadcasted_iota(jnp.int32, (LANES, LANES), 1)
    same_head = (brow // HEAD_DIM) == (bcol // HEAD_DIM)
    ones_bd = same_head.astype(F32)
    head0 = col < HEAD_DIM
    zeros_bd = jnp.zeros((LANES, LANES), BF16)

    def bd(x):
        xb = x.astype(BF16)
        return jnp.where(same_head, jnp.concatenate([xb, xb], axis=0), jnp.zeros((), BF16))

    def tile(ref, c, p):
        return ref[c * L:(c + 1) * L, p * LANES:(p + 1) * LANES]

    def vec(ref, p):
        return ref[:, p * LANES:(p + 1) * LANES]

    def stacked_dot(xs, w):
        out = _dot(jnp.concatenate(xs, axis=0), w)
        return [out[i * L:(i + 1) * L] for i in range(len(xs))]

    lanes2 = lambda x, y: jnp.concatenate([x, y], axis=1)
    rows2 = lambda x, y: jnp.concatenate([x, y], axis=0)

    r = [tile(r_ref, c, p) for c, p in items]
    k = [tile(k_ref, c, p) for c, p in items]
    v = [tile(v_ref, c, p) for c, p in items]
    lw = [tile(lw_ref, c, p) for c, p in items]
    a = [tile(a_ref, c, p) for c, p in items]
    if has_vmix:
        v = [vi + (tile(vf_ref, c, p) - vi) * tile(vg_ref, c, p) for vi, (c, p) in zip(v, items)]
    kx = [ki * vec(kk_ref, p) for ki, (c, p) in zip(k, items)]
    ss = stacked_dot([x * x for x in kx], ones_bd)
    kk = [x * lax.rsqrt(jnp.maximum(s, 1e-24)) for x, s in zip(kx, ss)]
    kh = [ki * (1.0 + (ai - 1.0) * vec(ka_ref, p)) for ki, ai, (c, p) in zip(k, a, items)]
    b_s = [x * ai for x, ai in zip(kk, a)]

    lw_hi = [x.astype(BF16) for x in lw]
    lw_lo = [(x - h.astype(F32)).astype(BF16) for x, h in zip(lw, lw_hi)]
    cum = [_dot(tri2, rows2(h, l)) for h, l in zip(lw_hi, lw_lo)]
    cum_last = [x[L - 1:L, :] for x in cum]
    e_fwd = [jnp.exp(x) for x in cum]
    e_inv = [jnp.exp(-x) for x in cum]
    e_tail = [jnp.exp(xl - x) for x, xl in zip(cum, cum_last)]
    r_t = [x * e for x, e in zip(r, e_fwd)]
    a_t = [-x * jnp.exp(c_ - l_) for x, c_, l_ in zip(kk, cum, lw)]
    b_t = [x * e for x, e in zip(b_s, e_inv)]
    k_t = [x * e for x, e in zip(kh, e_inv)]
    b_hat = [x * e for x, e in zip(b_s, e_tail)]
    k_hat = [x * e for x, e in zip(kh, e_tail)]

    a_all = [_dot_nt(rows2(at, rt),
                     jnp.concatenate([jnp.where(head0, bt, 0.0), jnp.where(head0, 0.0, bt),
                                      jnp.where(head0, kt, 0.0), jnp.where(head0, 0.0, kt)], axis=0))
             for at, rt, bt, kt in zip(a_t, r_t, b_t, k_t)]
    a_ab = [jnp.where(strict, x[:L, :LANES], 0.0) for x in a_all]
    a_ak = [jnp.where(strict, x[:L, LANES:], 0.0) for x in a_all]
    a_rb = [jnp.where(incl, x[L:, :LANES], 0.0) for x in a_all]
    a_rk = [jnp.where(incl, x[L:, LANES:], 0.0) for x in a_all]

    v_bd = [bd(x) for x in v]
    akv = [_dot(x, w) for x, w in zip(a_ak, v_bd)]
    pw = [_dot(x, bd(x)) for x in a_ab]
    t_inv = [eye2 + x for x in a_ab]
    for _ in range(int(math.log2(L)) - 2):
        res = [_dot(p_, lanes2(bd(t_), bd(p_))) for p_, t_ in zip(pw, t_inv)]
        t_inv = [t_ + x[:, :LANES] for t_, x in zip(t_inv, res)]
        pw = [x[:, LANES:] for x in res]
    t_inv = [t_ + _dot(p_, bd(t_)) for p_, t_ in zip(pw, t_inv)]
    au = [_dot(t_, lanes2(bd(at), bd(x))) for t_, at, x in zip(t_inv, a_t, akv)]
    a2 = [x[:, :LANES] for x in au]
    u0 = [x[:, LANES:] for x in au]
    qy = [_dot(lanes2(rb, rk_),
               rows2(lanes2(bd(a2_), bd(u0_)), lanes2(zeros_bd, vb)))
          for rb, rk_, a2_, u0_, vb in zip(a_rb, a_rk, a2, u0, v_bd)]
    q = [rt + x[:, :LANES] for rt, x in zip(r_t, qy)]
    y0 = [x[:, LANES:] for x in qy]
    rkr = stacked_dot([ri * khi * vec(rk_ref, p) for ri, khi, (c, p) in zip(r, kh, items)], ones_bd)
    bonus = [x * vi for x, vi in zip(rkr, v)]

    s = [s_ref[p] for p in range(n_pairs)]
    y = []
    for c in range(n_chunks):
        idx = [c * n_pairs + p for p in range(n_pairs)]
        fs = [_dot_nt(rows2(a2[i], q[i]), s[p]) for p, i in enumerate(idx)]
        u = [u0[i] + f[:L] for f, i in zip(fs, idx)]
        y += [y0[i] + f[L:] for f, i in zip(fs, idx)]
        upd = [_dot_tn(rows2(ui, v[i]), rows2(b_hat[i], k_hat[i])) for ui, i in zip(u, idx)]
        s = [s[p] * jnp.exp(cum_last[i]) + jnp.where(same_head, upd[p], 0.0)
             for p, i in enumerate(idx)]
    for p in range(n_pairs):
        s_ref[p] = s[p]

    mean = stacked_dot(y, ones_bd)
    dev = [yi - m * (1.0 / HEAD_DIM) for yi, m in zip(y, mean)]
    var = stacked_dot([x * x for x in dev], ones_bd)
    for i, (c, p) in enumerate(items):
        yn = dev[i] * lax.rsqrt(var[i] * (1.0 / HEAD_DIM) + GN_EPS) * vec(gg_ref, p) + vec(gb_ref, p)
        y_ref[c * L:(c + 1) * L, p * LANES:(p + 1) * LANES] = yn + bonus[i]


def wkv7(rkv, lag, vmix, k_k, k_a, r_k, gn_g, gn_b, seq, *, n_chunks=1):
    t = rkv.shape[0]
    d = rkv.shape[1] // 3
    n_pairs = d // LANES
    gw = n_pairs * LANES
    rows = n_chunks * CHUNK
    n_grp = d // gw
    grid = (t // seq, n_grp, seq // rows)
    steps = seq // rows
    tok_at = lambda part: pl.BlockSpec((rows, gw), lambda b, g, c: (b * steps + c, part * n_grp + g))
    tok = tok_at(0)
    vec = pl.BlockSpec((1, gw), lambda b, g, c: (0, g))
    toks = [rkv, rkv, rkv, lag, lag] + (list(vmix) if vmix is not None else [])
    tok_specs = ([tok_at(0), tok_at(1), tok_at(2), tok_at(0), tok_at(1)]
                 + ([tok_at(2), tok] if vmix is not None else []))
    vecs = [k_k, k_a, r_k, gn_g, gn_b]
    return pl.pallas_call(
        functools.partial(_wkv_kernel, n_pairs=n_pairs, n_chunks=n_chunks,
                          has_vmix=vmix is not None),
        out_shape=jax.ShapeDtypeStruct((t, d), F32),
        grid=grid,
        in_specs=tok_specs + [vec] * len(vecs),
        out_specs=tok,
        scratch_shapes=[pltpu.VMEM((n_pairs, LANES, LANES), F32)],
        compiler_params=_params(3), name="wkv7",
    )(*toks, *vecs)


def _swa_kernel(q_ref, kp_ref, kc_ref, vp_ref, vc_ref, sink_ref, o_ref, *, blocks_per_seq, n_kv):
    n = pl.program_id(0) % blocks_per_seq
    w = WINDOW
    qi = lax.broadcasted_iota(jnp.int32, (w, 2 * w), 0)
    ki = lax.broadcasted_iota(jnp.int32, (w, 2 * w), 1)
    rel = w + qi - ki
    valid = (rel >= 0) & (rel < w) & ((ki >= w) | (n > 0))
    lane = lax.broadcasted_iota(jnp.int32, (2 * w, LANES), 1)
    low = lane < HEAD_DIM
    low_o = lax.broadcasted_iota(jnp.int32, (w, LANES), 1) < HEAD_DIM
    pairs = range(n_kv // 2)
    tiles = [(g, half) for g in range(GROUP) for half in range(2)]

    def band(p_ref, c_ref, j):
        cols = slice(j * LANES, (j + 1) * LANES)
        b = jnp.concatenate([p_ref[:, cols], c_ref[:, cols]], axis=0).astype(BF16)
        zero = jnp.zeros((), BF16)
        return jnp.concatenate([jnp.where(low, b, zero), jnp.where(low, zero, b)], axis=0)

    k2 = [band(kp_ref, kc_ref, j) for j in pairs]
    v2 = [band(vp_ref, vc_ref, j) for j in pairs]
    q4 = [jnp.concatenate([q_ref[:, (j * GROUP + g) * LANES:(j * GROUP + g + 1) * LANES]
                           for g in range(GROUP)], axis=0) for j in pairs]
    sc = [_dot_nt(q, k) * (HEAD_DIM ** -0.5) for q, k in zip(q4, k2)]
    sink = [[sink_ref[(2 * j + half) * GROUP + g] for g, half in tiles] for j in pairs]
    s = [[jnp.where(valid, x[g * w:(g + 1) * w, half * 2 * w:(half + 1) * 2 * w], -jnp.inf)
          for g, half in tiles] for x in sc]
    m = [[jnp.maximum(jnp.max(x, axis=-1, keepdims=True), sk) for x, sk in zip(xs, sks)]
         for xs, sks in zip(s, sink)]
    e = [[jnp.exp(x - mx) for x, mx in zip(xs, ms)] for xs, ms in zip(s, m)]
    inv = [[1.0 / (jnp.sum(x, axis=-1, keepdims=True) + jnp.exp(sk - mx))
            for x, sk, mx in zip(xs, sks, ms)] for xs, sks, ms in zip(e, sink, m)]
    e4 = [jnp.concatenate([jnp.concatenate([xs[2 * g], xs[2 * g + 1]], axis=1)
                           for g in range(GROUP)], axis=0) for xs in e]
    out = [_dot(x, v) for x, v in zip(e4, v2)]
    for j in pairs:
        for g in range(GROUP):
            scale = jnp.where(low_o, inv[j][2 * g], inv[j][2 * g + 1])
            o_ref[:, (j * GROUP + g) * LANES:(j * GROUP + g + 1) * LANES] = (
                out[j][g * w:(g + 1) * w] * scale)


def paired_head_order(n_kv):
    return [(2 * j + half) * GROUP + g
            for j in range(n_kv // 2) for g in range(GROUP) for half in range(2)]


def swa_sink_attention(q, kv, sinks, seq):
    t, dq = q.shape
    dkv = kv.shape[1] // 2
    nb = t // WINDOW
    cur = lambda i: (i, 0)
    prev = lambda i: (jnp.maximum(i - 1, 0), 0)
    cur_v = lambda i: (i, 1)
    prev_v = lambda i: (jnp.maximum(i - 1, 0), 1)
    return pl.pallas_call(
        functools.partial(_swa_kernel, blocks_per_seq=seq // WINDOW, n_kv=dkv // HEAD_DIM),
        out_shape=jax.ShapeDtypeStruct((t, dq), F32),
        grid=(nb,),
        in_specs=[pl.BlockSpec((WINDOW, dq), cur),
                  pl.BlockSpec((WINDOW, dkv), prev), pl.BlockSpec((WINDOW, dkv), cur),
                  pl.BlockSpec((WINDOW, dkv), prev_v), pl.BlockSpec((WINDOW, dkv), cur_v),
                  pl.BlockSpec(memory_space=pltpu.SMEM)],
        out_specs=pl.BlockSpec((WINDOW, dq), cur),
        compiler_params=_params(1), name="swa_sink_attention",
    )(q, kv, kv, kv, kv, sinks)


def _rope(t, cos, sin_signed):
    n = t.shape[1]
    reps = n // LANES
    cos = jnp.concatenate([cos] * reps, axis=1) if reps > 1 else cos
    sin_signed = jnp.concatenate([sin_signed] * reps, axis=1) if reps > 1 else sin_signed
    lane = lax.broadcasted_iota(jnp.int32, t.shape, 1)
    half = HEAD_DIM // 2
    partner = jnp.where(lane % HEAD_DIM < half, pltpu.roll(t, n - half, axis=1),
                        pltpu.roll(t, half, axis=1))
    return t * cos + partner * sin_signed


def _rope_tables(positions):
    inv_freq = ROPE_THETA ** (-jnp.arange(0, HEAD_DIM, 2, dtype=F32) / HEAD_DIM)
    ang = positions.astype(F32).reshape(-1, 1) * inv_freq
    cos, sin = jnp.cos(ang), jnp.sin(ang)
    reps = LANES // HEAD_DIM
    return (jnp.concatenate([cos, cos] * reps, axis=1),
            jnp.concatenate([-sin, sin] * reps, axis=1))


def kernel(x, p, positions, norm_g, ffn_w_gate, ffn_w_up, ffn_w_down, ple_w_up, ple_w_gate, rwkv_mu, rwkv_w_rkv, rwkv_w_o, rwkv_w0, rwkv_w1, rwkv_w2, rwkv_a0, rwkv_a1, rwkv_a2, rwkv_v0, rwkv_v1, rwkv_v2, rwkv_g1, rwkv_g2, rwkv_k_k, rwkv_k_a, rwkv_r_k, rwkv_gn_g, rwkv_gn_b, kv_norm_g, w_kv, attn_w_q, attn_w_o, attn_sinks, final_norm_g):
    bsz, seq, d = x.shape
    depth = norm_g.shape[0]
    n_a = rwkv_mu.shape[0]
    t = bsz * seq
    x = x.reshape(t, d)
    cos, sin_signed = _rope_tables(positions)
    row = lambda v: v.reshape(1, -1)
    ident = lambda z: z
    tm = 512
    tok_tile = lambda width: ((tm, width), lambda i, j: (i, j))

    wg_all = ffn_w_gate.astype(BF16)
    wu_all = ffn_w_up.astype(BF16)
    wd_all = (0.5 * ffn_w_down).astype(BF16)
    w_rkv_all = rwkv_w_rkv.astype(BF16)
    w_ro_all = rwkv_w_o.astype(BF16)
    w_pg_all = ple_w_gate.astype(BF16)
    n_q = attn_sinks.shape[1]
    order = jnp.asarray(paired_head_order(n_q // GROUP))
    n_b = attn_w_q.shape[0]
    w_q_all = attn_w_q.reshape(n_b, d, n_q, HEAD_DIM)[:, :, order].reshape(n_b, d, -1).astype(BF16)
    w_o_all = attn_w_o.reshape(n_b, n_q, HEAD_DIM, d)[:, order].reshape(n_b, -1, d).astype(BF16)

    rkv_first = None
    kv = None
    for i in range(depth):
        if i == n_a:
            n_kv_half = w_kv.shape[1] // 2
            kv = fused_matmul(
                [x, row(kv_norm_g)], w_kv.astype(BF16),
                [(cos, (tm, LANES), lambda i_, j: (i_, 0)),
                 (sin_signed, (tm, LANES), lambda i_, j: (i_, 0))],
                _rms,
                lambda acc, c, s: jnp.where(pl.program_id(1) == 0, _rope(acc, c, s), acc),
                1, w_kv.shape[1], tm=tm, tn=n_kv_half, name="kv_proj")

        x = ffn_half_step(x, row(norm_g[i, 0]), wg_all, wu_all, wd_all, (i, 0))

        if i < n_a:
            j = i
            loras = [(rwkv_w1[j], rwkv_w2[j], rwkv_w0[j]), (rwkv_a1[j], rwkv_a2[j], rwkv_a0[j]),
                     (rwkv_g1[j], rwkv_g2[j], None)]
            if j > 0:
                loras.append((rwkv_v1[j - 1], rwkv_v2[j - 1], rwkv_v0[j - 1]))
            rkv, lag, *vg = rwkv_in(x, norm_g[i, 1], rwkv_mu[j], w_rkv_all, j, loras, seq)
            if j == 0:
                rkv_first = rkv
            vmix = (rkv_first, vg[0]) if vg else None
            y = wkv7(rkv, lag, vmix, row(rwkv_k_k[j]), row(rwkv_k_a[j]),
                     row(rwkv_r_k[j]), row(rwkv_gn_g[j]), row(rwkv_gn_b[j]), seq)
            x = fused_matmul([y, (lag, 2)], w_ro_all, [(x,) + tok_tile(d)],
                             lambda y_, g_: y_ * g_, lambda acc, res: res + acc,
                             1, d, tm=tm, tn=d, name="rwkv_out_proj", lead=(j,))
        else:
            j = i - n_a
            q = fused_matmul(
                [x, row(norm_g[i, 1])], w_q_all,
                [(cos, (tm, LANES), lambda i_, j_: (i_, 0)),
                 (sin_signed, (tm, LANES), lambda i_, j_: (i_, 0))],
                _rms, _rope, 1, d, tm=tm, tn=d, name="attn_q_proj", lead=(j,))
            o = swa_sink_attention(q, kv, attn_sinks[j], seq)
            x = fused_matmul([o], w_o_all, [(x,) + tok_tile(d)],
                             ident, lambda acc, res: res + acc,
                             1, d, tm=tm, tn=d, name="attn_out_proj", lead=(j,))

        x = ffn_half_step(x, row(norm_g[i, 2]), wg_all, wu_all, wd_all, (i, 1))

        last = i == depth - 1
        fin = row(final_norm_g)

        def ple_epilogue(acc, res, p_tile, w_up, fin_g, last=last):
            out = res + _sigmoid(acc) * _dot(p_tile.astype(BF16), w_up)
            return _rms(out, fin_g) if last else out

        x = fused_matmul(
            [x, row(norm_g[i, 3])], w_pg_all,
            [(x,) + tok_tile(d),
             (p.reshape(depth, t, -1), (None, tm, p.shape[-1]), lambda i_, j_, i=i: (i, i_, 0)),
             (ple_w_up[i].astype(BF16), (p.shape[-1], d), lambda i_, j_: (0, 0)),
             (fin, (1, d), lambda i_, j_: (0, 0))],
            _rms, ple_epilogue, 1, d, tm=tm, tn=d, name="ple", lead=(i,))
    return x.reshape(bsz, seq, d)
```

```python
import functools
import math

import jax
import jax.numpy as jnp
from jax import lax
from jax.experimental import pallas as pl
from jax.experimental.pallas import tpu as pltpu

F32 = jnp.float32
BF16 = jnp.bfloat16

HEAD_DIM = 64
LANES = 128
CHUNK = 64
WINDOW = 128
GROUP = 4
ROPE_THETA = 10000.0
RMS_EPS = 1e-6
GN_EPS = 64e-5
DECAY_SCALE = math.exp(-0.5)
VMEM_LIMIT = 56 * 1024 * 1024


def _params(n_axes, **kw):
    return pltpu.CompilerParams(dimension_semantics=("arbitrary",) * n_axes,
                                vmem_limit_bytes=VMEM_LIMIT, **kw)


def _rms(x, g):
    return x * lax.rsqrt(jnp.mean(x * x, axis=-1, keepdims=True) + RMS_EPS) * g


def _sigmoid(x):
    return 1.0 / (1.0 + jnp.exp(-x))


def _dot(a, b):
    return jnp.dot(a.astype(BF16), b.astype(BF16), preferred_element_type=F32)


def _dot_nt(a, b):
    return lax.dot_general(a.astype(BF16), b.astype(BF16), (((1,), (1,)), ((), ())),
                           preferred_element_type=F32)


def _dot_tn(a, b):
    return lax.dot_general(a.astype(BF16), b.astype(BF16), (((0,), (0,)), ((), ())),
                           preferred_element_type=F32)


def _fmm_kernel(*refs, n_row, n_extra, n_out, prologue, epilogue):
    row = refs[:n_row]
    w_ref = refs[n_row]
    extra = refs[n_row + 1:n_row + 1 + n_extra]
    outs = refs[n_row + 1 + n_extra:n_row + 1 + n_extra + n_out]
    lhs_ref = refs[-1]

    @pl.when(pl.program_id(1) == 0)
    def _():
        lhs_ref[...] = prologue(*[r[...] for r in row]).astype(BF16)

    acc = _dot(lhs_ref[...], w_ref[...])
    res = epilogue(acc, *[e[...] for e in extra])
    if n_out == 1:
        res = (res,)
    for o, v in zip(outs, res):
        o[...] = v


def fused_matmul(rows, w, extras, prologue, epilogue, n_out, out_width, *, tm, tn, name, lead=()):
    k, n = w.shape[-2:]
    in_specs = []
    row_args = []
    for r in rows:
        if isinstance(r, tuple):
            arr, part = r
            in_specs.append(pl.BlockSpec((tm, k), lambda i, j, part=part: (i, part)))
        elif r.shape[0] == 1:
            arr = r
            in_specs.append(pl.BlockSpec((1, r.shape[1]), lambda i, j: (0, 0)))
        else:
            arr = r
            in_specs.append(pl.BlockSpec((tm, r.shape[1]), lambda i, j: (i, 0)))
        row_args.append(arr)
    rows = row_args
    t = max(r.shape[0] for r in rows)
    grid = (t // tm, n // tn)
    w_block = (None,) * len(lead) + (k, tn)
    if tn == n:
        in_specs.append(pl.BlockSpec(w_block, lambda i, j: lead + (0, 0),
                                     pipeline_mode=pl.Buffered(1)))
    else:
        in_specs.append(pl.BlockSpec(w_block, lambda i, j: lead + (0, j)))
    for arr, bs, im in extras:
        in_specs.append(pl.BlockSpec(bs, im))
    otn = out_width // (n // tn)
    out_shape = [jax.ShapeDtypeStruct((t, out_width), F32)] * n_out
    out_specs = [pl.BlockSpec((tm, otn), lambda i, j: (i, j))] * n_out
    res = pl.pallas_call(
        functools.partial(_fmm_kernel, n_row=len(rows), n_extra=len(extras), n_out=n_out,
                          prologue=prologue, epilogue=epilogue),
        out_shape=out_shape, grid=grid, in_specs=in_specs, out_specs=out_specs,
        scratch_shapes=[pltpu.VMEM((tm, k), BF16)],
        compiler_params=_params(2), name=name,
    )(*rows, w, *[e[0] for e in extras])
    return res[0] if n_out == 1 else res


def _ffn_kernel(x_ref, g_ref, wg_ref, wu_ref, wd_ref, o_ref, h_ref):
    f = pl.program_id(1)

    @pl.when(f == 0)
    def _():
        x = x_ref[...]
        h_ref[...] = _rms(x, g_ref[...]).astype(BF16)
        o_ref[...] = x

    h = h_ref[...]
    gate = _dot(h, wg_ref[...])
    up = _dot(h, wu_ref[...])
    act = (gate * _sigmoid(gate) * up).astype(BF16)
    o_ref[...] += _dot(act, wd_ref[...])


def ffn_half_step(x, g, wg, wu, wd_half, lead, *, tm=1024, tf=512):
    t, d = x.shape
    f = wg.shape[-1]
    none = (None,) * len(lead)
    return pl.pallas_call(
        _ffn_kernel,
        out_shape=jax.ShapeDtypeStruct((t, d), F32),
        grid=(t // tm, f // tf),
        in_specs=[pl.BlockSpec((tm, d), lambda i, j: (i, 0)),
                  pl.BlockSpec((1, d), lambda i, j: (0, 0)),
                  pl.BlockSpec(none + (d, tf), lambda i, j: lead + (0, j)),
                  pl.BlockSpec(none + (d, tf), lambda i, j: lead + (0, j)),
                  pl.BlockSpec(none + (tf, d), lambda i, j: lead + (j, 0))],
        out_specs=pl.BlockSpec((tm, d), lambda i, j: (i, 0)),
        scratch_shapes=[pltpu.VMEM((tm, d), BF16)],
        compiler_params=_params(2), name="ffn_half_step",
    )(x, g, wg, wu, wd_half)


def _rwkv_in_kernel(*refs, tiles_per_seq, n_col, has_vgate):
    x_ref, g_ref, mu_ref, w_ref, w1w_ref, w1a_ref, w1g_ref = refs[:7]
    rest = refs[7:]
    if has_vgate:
        w1v_ref, w2_ref, b_ref, w2v_ref, bv_ref, rkv_ref, lag_ref, vg_ref = rest[:8]
        xm_ref, carry_ref, mid_ref, midv_ref = rest[8:]
    else:
        w2_ref, b_ref, rkv_ref, lag_ref = rest[:4]
        xm_ref, carry_ref, mid_ref = rest[4:]
    i = pl.program_id(0)
    j = pl.program_id(1)

    @pl.when((i == 0) & (j == 0))
    def _():
        carry_ref[...] = jnp.zeros_like(carry_ref)

    @pl.when(j == 0)
    def _():
        h = _rms(x_ref[...], g_ref[...])
        tm = h.shape[0]
        last = carry_ref[7:8, :]
        first = jnp.where(i % tiles_per_seq == 0, jnp.zeros_like(last), last)
        row = lax.broadcasted_iota(jnp.int32, h.shape, 0)
        dx = jnp.where(row == 0, first, pltpu.roll(h, 1, axis=0)) - h
        carry_ref[...] = h[tm - 8:, :]

        def mix(c):
            return (h + dx * mu_ref[c:c + 1, :]).astype(BF16)

        xm_ref[0] = mix(0)
        xm_ref[1] = mix(2)
        xv = mix(3)
        xm_ref[2] = xv
        mid_ref[0] = jnp.tanh(_dot(mix(1), w1w_ref[...])).astype(BF16)
        mid_ref[1] = _dot(mix(4), w1a_ref[...]).astype(BF16)
        mid_ref[2] = _sigmoid(_dot(mix(5), w1g_ref[...])).astype(BF16)
        if has_vgate:
            midv_ref[...] = _dot(xv, w1v_ref[...]).astype(BF16)

    c = j // n_col
    rkv_ref[...] = _dot(xm_ref[c], w_ref[...])
    z = _dot(mid_ref[c], w2_ref[...]) + b_ref[...]
    sg = _sigmoid(z)
    lag_ref[...] = jnp.where(c == 0, -DECAY_SCALE * sg, jnp.where(c == 1, sg, z))

    if has_vgate:
        @pl.when(c == 0)
        def _():
            vg_ref[...] = _sigmoid(_dot(midv_ref[...], w2v_ref[...]) + bv_ref[...])


def _pad_rank(w1, w2, rank):
    r = w1.shape[1]
    return (jnp.pad(w1, ((0, 0), (0, rank - r))).astype(BF16),
            jnp.pad(w2, ((0, rank - r), (0, 0))).astype(BF16))


def _lane_multiple(n):
    return -(-n // LANES) * LANES


def rwkv_in(x, g, mu, w_rkv, layer, loras, seq, *, tm=512, tn=1024):
    t, d = x.shape
    tn = min(tn, d)
    n_col = d // tn
    has_vgate = len(loras) == 4
    const = lambda shape: pl.BlockSpec(shape, lambda i, j: (0,) * len(shape),
                                       pipeline_mode=pl.Buffered(1))
    rank = max(_lane_multiple(w1.shape[1]) for w1, _, _ in loras[:3])
    padded = [_pad_rank(w1, w2, rank) for w1, w2, _ in loras[:3]]
    w2_all = jnp.stack([w2 for _, w2 in padded])
    b_all = jnp.stack([jnp.zeros((1, d), F32) if b is None else b.reshape(1, d)
                       for _, _, b in loras[:3]])
    args = [x, g.reshape(1, d), mu, w_rkv] + [w1 for w1, _ in padded]
    in_specs = [pl.BlockSpec((tm, d), lambda i, j: (i, 0)), const((1, d)), const((6, d)),
                pl.BlockSpec((None, None, d, tn), lambda i, j: (layer, j // n_col, 0, j % n_col))]
    in_specs += [const((d, rank))] * 3
    by_branch = lambda rows: pl.BlockSpec((None, rows, tn), lambda i, j: (j // n_col, 0, j % n_col))
    col = lambda rows: pl.BlockSpec((rows, tn), lambda i, j: (0, j % n_col))
    out_shape = [jax.ShapeDtypeStruct((t, 3 * d), F32)] * 2
    out_specs = [pl.BlockSpec((tm, tn), lambda i, j: (i, j))] * 2
    scratch = [pltpu.VMEM((3, tm, d), BF16), pltpu.VMEM((8, d), F32), pltpu.VMEM((3, tm, rank), BF16)]
    if has_vgate:
        w1v, w2v, bv = loras[3]
        rank_v = _lane_multiple(w1v.shape[1])
        w1v, w2v = _pad_rank(w1v, w2v, rank_v)
        args += [w1v, w2_all, b_all, w2v, bv.reshape(1, d)]
        in_specs += [const((d, rank_v)), by_branch(rank), by_branch(1), col(rank_v), col(1)]
        out_shape.append(jax.ShapeDtypeStruct((t, d), F32))
        out_specs.append(pl.BlockSpec((tm, tn), lambda i, j: (i, jnp.minimum(j, n_col - 1))))
        scratch.append(pltpu.VMEM((tm, rank_v), BF16))
    else:
        args += [w2_all, b_all]
        in_specs += [by_branch(rank), by_branch(1)]
    return pl.pallas_call(
        functools.partial(_rwkv_in_kernel, tiles_per_seq=seq // tm, n_col=n_col,
                          has_vgate=has_vgate),
        out_shape=out_shape,
        grid=(t // tm, 3 * n_col),
        in_specs=in_specs,
        out_specs=out_specs,
        scratch_shapes=scratch,
        compiler_params=_params(2), name="rwkv_in",
    )(*args)


def _wkv_kernel(*refs, n_pairs, n_chunks, has_vmix):
    if has_vmix:
        (r_ref, k_ref, v_ref, lw_ref, a_ref, vf_ref, vg_ref,
         kk_ref, ka_ref, rk_ref, gg_ref, gb_ref, y_ref, s_ref) = refs
    else:
        (r_ref, k_ref, v_ref, lw_ref, a_ref,
         kk_ref, ka_ref, rk_ref, gg_ref, gb_ref, y_ref, s_ref) = refs

    @pl.when(pl.program_id(2) == 0)
    def _():
        s_ref[...] = jnp.zeros_like(s_ref)

    L = CHUNK
    items = [(c, p) for c in range(n_chunks) for p in range(n_pairs)]
    row = lax.broadcasted_iota(jnp.int32, (L, LANES), 0)
    col = lax.broadcasted_iota(jnp.int32, (L, LANES), 1)
    strict = (col % L) < row
    incl = (col % L) <= row
    eye2 = ((col % L) == row).astype(F32)
    tri = (lax.broadcasted_iota(jnp.int32, (L, L), 1)
           <= lax.broadcasted_iota(jnp.int32, (L, L), 0)).astype(BF16)
    tri2 = jnp.concatenate([tri, tri], axis=1)
    brow = lax.broadcasted_iota(jnp.int32, (LANES, LANES), 0)
    bcol = lax.broadcasted_iota(jnp.int32, (LANES, LANES), 1)
    same_head = (brow // HEAD_DIM) == (bcol // HEAD_DIM)
    ones_bd = same_head.astype(F32)
    head0 = col < HEAD_DIM
    zeros_bd = jnp.zeros((LANES, LANES), BF16)

    def bd(x):
        xb = x.astype(BF16)
        return jnp.where(same_head, jnp.concatenate([xb, xb], axis=0), jnp.zeros((), BF16))

    def tile(ref, c, p):
        return ref[c * L:(c + 1) * L, p * LANES:(p + 1) * LANES]

    def vec(ref, p):
        return ref[:, p * LANES:(p + 1) * LANES]

    def stacked_dot(xs, w):
        out = _dot(jnp.concatenate(xs, axis=0), w)
        return [out[i * L:(i + 1) * L] for i in range(len(xs))]

    lanes2 = lambda x, y: jnp.concatenate([x, y], axis=1)
    rows2 = lambda x, y: jnp.concatenate([x, y], axis=0)

    r = [tile(r_ref, c, p) for c, p in items]
    k = [tile(k_ref, c, p) for c, p in items]
    v = [tile(v_ref, c, p) for c, p in items]
    lw = [tile(lw_ref, c, p) for c, p in items]
    a = [tile(a_ref, c, p) for c, p in items]
    if has_vmix:
        v = [vi + (tile(vf_ref, c, p) - vi) * tile(vg_ref, c, p) for vi, (c, p) in zip(v, items)]
    kx = [ki * vec(kk_ref, p) for ki, (c, p) in zip(k, items)]
    ss = stacked_dot([x * x for x in kx], ones_bd)
    kk = [x * lax.rsqrt(jnp.maximum(s, 1e-24)) for x, s in zip(kx, ss)]
    kh = [ki * (1.0 + (ai - 1.0) * vec(ka_ref, p)) for ki, ai, (c, p) in zip(k, a, items)]
    b_s = [x * ai for x, ai in zip(kk, a)]

    lw_hi = [x.astype(BF16) for x in lw]
    lw_lo = [(x - h.astype(F32)).astype(BF16) for x, h in zip(lw, lw_hi)]
    cum = [_dot(tri2, rows2(h, l)) for h, l in zip(lw_hi, lw_lo)]
    cum_last = [x[L - 1:L, :] for x in cum]
    e_fwd = [jnp.exp(x) for x in cum]
    e_inv = [jnp.exp(-x) for x in cum]
    e_tail = [jnp.exp(xl - x) for x, xl in zip(cum, cum_last)]
    r_t = [x * e for x, e in zip(r, e_fwd)]
    a_t = [-x * jnp.exp(c_ - l_) for x, c_, l_ in zip(kk, cum, lw)]
    b_t = [x * e for x, e in zip(b_s, e_inv)]
    k_t = [x * e for x, e in zip(kh, e_inv)]
    b_hat = [x * e for x, e in zip(b_s, e_tail)]
    k_hat = [x * e for x, e in zip(kh, e_tail)]

    a_all = [_dot_nt(rows2(at, rt),
                     jnp.concatenate([jnp.where(head0, bt, 0.0), jnp.where(head0, 0.0, bt),
                                      jnp.where(head0, kt, 0.0), jnp.where(head0, 0.0, kt)], axis=0))
             for at, rt, bt, kt in zip(a_t, r_t, b_t, k_t)]
    a_ab = [jnp.where(strict, x[:L, :LANES], 0.0) for x in a_all]
    a_ak = [jnp.where(strict, x[:L, LANES:], 0.0) for x in a_all]
    a_rb = [jnp.where(incl, x[L:, :LANES], 0.0) for x in a_all]
    a_rk = [jnp.where(incl, x[L:, LANES:], 0.0) for x in a_all]

    v_bd = [bd(x) for x in v]
    akv = [_dot(x, w) for x, w in zip(a_ak, v_bd)]
    pw = [_dot(x, bd(x)) for x in a_ab]
    t_inv = [eye2 + x for x in a_ab]
    for _ in range(int(math.log2(L)) - 2):
        res = [_dot(p_, lanes2(bd(t_), bd(p_))) for p_, t_ in zip(pw, t_inv)]
        t_inv = [t_ + x[:, :LANES] for t_, x in zip(t_inv, res)]
        pw = [x[:, LANES:] for x in res]
    t_inv = [t_ + _dot(p_, bd(t_)) for p_, t_ in zip(pw, t_inv)]
    au = [_dot(t_, lanes2(bd(at), bd(x))) for t_, at, x in zip(t_inv, a_t, akv)]
    a2 = [x[:, :LANES] for x in au]
    u0 = [x[:, LANES:] for x in au]
    qy = [_dot(lanes2(rb, rk_),
               rows2(lanes2(bd(a2_), bd(u0_)), lanes2(zeros_bd, vb)))
          for rb, rk_, a2_, u0_, vb in zip(a_rb, a_rk, a2, u0, v_bd)]
    q = [rt + x[:, :LANES] for rt, x in zip(r_t, qy)]
    y0 = [x[:, LANES:] for x in qy]
    rkr = stacked_dot([ri * khi * vec(rk_ref, p) for ri, khi, (c, p) in zip(r, kh, items)], ones_bd)
    bonus = [x * vi for x, vi in zip(rkr, v)]

    s = [s_ref[p] for p in range(n_pairs)]
    y = []
    for c in range(n_chunks):
        idx = [c * n_pairs + p for p in range(n_pairs)]
        fs = [_dot_nt(rows2(a2[i], q[i]), s[p]) for p, i in enumerate(idx)]
        u = [u0[i] + f[:L] for f, i in zip(fs, idx)]
        y += [y0[i] + f[L:] for f, i in zip(fs, idx)]
        upd = [_dot_tn(rows2(ui, v[i]), rows2(b_hat[i], k_hat[i])) for ui, i in zip(u, idx)]
        s = [s[p] * jnp.exp(cum_last[i]) + jnp.where(same_head, upd[p], 0.0)
             for p, i in enumerate(idx)]
    for p in range(n_pairs):
        s_ref[p] = s[p]

    mean = stacked_dot(y, ones_bd)
    dev = [yi - m * (1.0 / HEAD_DIM) for yi, m in zip(y, mean)]
    var = stacked_dot([x * x for x in dev], ones_bd)
    for i, (c, p) in enumerate(items):
        yn = dev[i] * lax.rsqrt(var[i] * (1.0 / HEAD_DIM) + GN_EPS) * vec(gg_ref, p) + vec(gb_ref, p)
        y_ref[c * L:(c + 1) * L, p * LANES:(p + 1) * LANES] = yn + bonus[i]


def wkv7(rkv, lag, vmix, k_k, k_a, r_k, gn_g, gn_b, seq, *, n_chunks=1):
    t = rkv.shape[0]
    d = rkv.shape[1] // 3
    n_pairs = d // LANES
    gw = n_pairs * LANES
    rows = n_chunks * CHUNK
    n_grp = d // gw
    grid = (t // seq, n_grp, seq // rows)
    steps = seq // rows
    tok_at = lambda part: pl.BlockSpec((rows, gw), lambda b, g, c: (b * steps + c, part * n_grp + g))
    tok = tok_at(0)
    vec = pl.BlockSpec((1, gw), lambda b, g, c: (0, g))
    toks = [rkv, rkv, rkv, lag, lag] + (list(vmix) if vmix is not None else [])
    tok_specs = ([tok_at(0), tok_at(1), tok_at(2), tok_at(0), tok_at(1)]
                 + ([tok_at(2), tok] if vmix is not None else []))
    vecs = [k_k, k_a, r_k, gn_g, gn_b]
    return pl.pallas_call(
        functools.partial(_wkv_kernel, n_pairs=n_pairs, n_chunks=n_chunks,
                          has_vmix=vmix is not None),
        out_shape=jax.ShapeDtypeStruct((t, d), F32),
        grid=grid,
        in_specs=tok_specs + [vec] * len(vecs),
        out_specs=tok,
        scratch_shapes=[pltpu.VMEM((n_pairs, LANES, LANES), F32)],
        compiler_params=_params(3), name="wkv7",
    )(*toks, *vecs)


def _swa_kernel(q_ref, kp_ref, kc_ref, vp_ref, vc_ref, sink_ref, o_ref, *, blocks_per_seq, n_kv):
    n = pl.program_id(0) % blocks_per_seq
    w = WINDOW
    from_prev = (lax.broadcasted_iota(jnp.int32, (w, w), 1)
                 > lax.broadcasted_iota(jnp.int32, (w, w), 0))
    prev_fill = jnp.where(n > 0, 0.0, -jnp.inf)
    lane = lax.broadcasted_iota(jnp.int32, (2 * w, LANES), 1)
    low = lane < HEAD_DIM
    low_o = lax.broadcasted_iota(jnp.int32, (w, LANES), 1) < HEAD_DIM
    pairs = range(n_kv // 2)
    tiles = [(g, half) for g in range(GROUP) for half in range(2)]

    def band(p_ref, c_ref, j):
        cols = slice(j * LANES, (j + 1) * LANES)
        b = jnp.concatenate([p_ref[:, cols], c_ref[:, cols]], axis=0).astype(BF16)
        zero = jnp.zeros((), BF16)
        return jnp.concatenate([jnp.where(low, b, zero), jnp.where(low, zero, b)], axis=0)

    k2 = [band(kp_ref, kc_ref, j) for j in pairs]
    v2 = [band(vp_ref, vc_ref, j) for j in pairs]
    q4 = [jnp.concatenate([q_ref[:, (j * GROUP + g) * LANES:(j * GROUP + g + 1) * LANES]
                           for g in range(GROUP)], axis=0) for j in pairs]
    sc = [_dot_nt(q, k) * (HEAD_DIM ** -0.5) for q, k in zip(q4, k2)]
    sink = [[sink_ref[(2 * j + half) * GROUP + g] for g, half in tiles] for j in pairs]
    def folded(x, g, half):
        r0, c0 = g * w, half * 2 * w
        return jnp.where(from_prev, x[r0:r0 + w, c0:c0 + w] + prev_fill, x[r0:r0 + w, c0 + w:c0 + 2 * w])

    def unfolded(x):
        return jnp.concatenate([jnp.where(from_prev, x, 0.0), jnp.where(from_prev, 0.0, x)], axis=1)

    s = [[folded(x, g, half) for g, half in tiles] for x in sc]
    m = [[jnp.maximum(jnp.max(x, axis=-1, keepdims=True), sk) for x, sk in zip(xs, sks)]
         for xs, sks in zip(s, sink)]
    e = [[jnp.exp(x - mx) for x, mx in zip(xs, ms)] for xs, ms in zip(s, m)]
    inv = [[1.0 / (jnp.sum(x, axis=-1, keepdims=True) + jnp.exp(sk - mx))
            for x, sk, mx in zip(xs, sks, ms)] for xs, sks, ms in zip(e, sink, m)]
    e4 = [jnp.concatenate([jnp.concatenate([unfolded(xs[2 * g]), unfolded(xs[2 * g + 1])], axis=1)
                           for g in range(GROUP)], axis=0) for xs in e]
    out = [_dot(x, v) for x, v in zip(e4, v2)]
    for j in pairs:
        for g in range(GROUP):
            scale = jnp.where(low_o, inv[j][2 * g], inv[j][2 * g + 1])
            o_ref[:, (j * GROUP + g) * LANES:(j * GROUP + g + 1) * LANES] = (
                out[j][g * w:(g + 1) * w] * scale)


def paired_head_order(n_kv):
    return [(2 * j + half) * GROUP + g
            for j in range(n_kv // 2) for g in range(GROUP) for half in range(2)]


def swa_sink_attention(q, kv, sinks, seq):
    t, dq = q.shape
    dkv = kv.shape[1] // 2
    nb = t // WINDOW
    cur = lambda i: (i, 0)
    prev = lambda i: (jnp.maximum(i - 1, 0), 0)
    cur_v = lambda i: (i, 1)
    prev_v = lambda i: (jnp.maximum(i - 1, 0), 1)
    return pl.pallas_call(
        functools.partial(_swa_kernel, blocks_per_seq=seq // WINDOW, n_kv=dkv // HEAD_DIM),
        out_shape=jax.ShapeDtypeStruct((t, dq), F32),
        grid=(nb,),
        in_specs=[pl.BlockSpec((WINDOW, dq), cur),
                  pl.BlockSpec((WINDOW, dkv), prev), pl.BlockSpec((WINDOW, dkv), cur),
                  pl.BlockSpec((WINDOW, dkv), prev_v), pl.BlockSpec((WINDOW, dkv), cur_v),
                  pl.BlockSpec(memory_space=pltpu.SMEM)],
        out_specs=pl.BlockSpec((WINDOW, dq), cur),
        compiler_params=_params(1), name="swa_sink_attention",
    )(q, kv, kv, kv, kv, sinks)


def _rope(t, cos, sin_signed):
    n = t.shape[1]
    reps = n // LANES
    cos = jnp.concatenate([cos] * reps, axis=1) if reps > 1 else cos
    sin_signed = jnp.concatenate([sin_signed] * reps, axis=1) if reps > 1 else sin_signed
    lane = lax.broadcasted_iota(jnp.int32, t.shape, 1)
    half = HEAD_DIM // 2
    partner = jnp.where(lane % HEAD_DIM < half, pltpu.roll(t, n - half, axis=1),
                        pltpu.roll(t, half, axis=1))
    return t * cos + partner * sin_signed


def _rope_tables(positions):
    inv_freq = ROPE_THETA ** (-jnp.arange(0, HEAD_DIM, 2, dtype=F32) / HEAD_DIM)
    ang = positions.astype(F32).reshape(-1, 1) * inv_freq
    cos, sin = jnp.cos(ang), jnp.sin(ang)
    reps = LANES // HEAD_DIM
    return (jnp.concatenate([cos, cos] * reps, axis=1),
            jnp.concatenate([-sin, sin] * reps, axis=1))


def kernel(x, p, positions, norm_g, ffn_w_gate, ffn_w_up, ffn_w_down, ple_w_up, ple_w_gate, rwkv_mu, rwkv_w_rkv, rwkv_w_o, rwkv_w0, rwkv_w1, rwkv_w2, rwkv_a0, rwkv_a1, rwkv_a2, rwkv_v0, rwkv_v1, rwkv_v2, rwkv_g1, rwkv_g2, rwkv_k_k, rwkv_k_a, rwkv_r_k, rwkv_gn_g, rwkv_gn_b, kv_norm_g, w_kv, attn_w_q, attn_w_o, attn_sinks, final_norm_g):
    bsz, seq, d = x.shape
    depth = norm_g.shape[0]
    n_a = rwkv_mu.shape[0]
    t = bsz * seq
    x = x.reshape(t, d)
    cos, sin_signed = _rope_tables(positions)
    row = lambda v: v.reshape(1, -1)
    ident = lambda z: z
    tm = 512
    tok_tile = lambda width: ((tm, width), lambda i, j: (i, j))

    wg_all = ffn_w_gate.astype(BF16)
    wu_all = ffn_w_up.astype(BF16)
    wd_all = (0.5 * ffn_w_down).astype(BF16)
    w_rkv_all = rwkv_w_rkv.astype(BF16)
    w_ro_all = rwkv_w_o.astype(BF16)
    w_pg_all = ple_w_gate.astype(BF16)
    n_q = attn_sinks.shape[1]
    order = jnp.asarray(paired_head_order(n_q // GROUP))
    n_b = attn_w_q.shape[0]
    w_q_all = attn_w_q.reshape(n_b, d, n_q, HEAD_DIM)[:, :, order].reshape(n_b, d, -1).astype(BF16)
    w_o_all = attn_w_o.reshape(n_b, n_q, HEAD_DIM, d)[:, order].reshape(n_b, -1, d).astype(BF16)

    rkv_first = None
    kv = None
    for i in range(depth):
        if i == n_a:
            n_kv_half = w_kv.shape[1] // 2
            kv = fused_matmul(
                [x, row(kv_norm_g)], w_kv.astype(BF16),
                [(cos, (tm, LANES), lambda i_, j: (i_, 0)),
                 (sin_signed, (tm, LANES), lambda i_, j: (i_, 0))],
                _rms,
                lambda acc, c, s: lax.cond(pl.program_id(1) == 0, lambda: _rope(acc, c, s),
                                           lambda: acc),
                1, w_kv.shape[1], tm=tm, tn=n_kv_half, name="kv_proj")

        x = ffn_half_step(x, row(norm_g[i, 0]), wg_all, wu_all, wd_all, (i, 0))

        if i < n_a:
            j = i
            loras = [(rwkv_w1[j], rwkv_w2[j], rwkv_w0[j]), (rwkv_a1[j], rwkv_a2[j], rwkv_a0[j]),
                     (rwkv_g1[j], rwkv_g2[j], None)]
            if j > 0:
                loras.append((rwkv_v1[j - 1], rwkv_v2[j - 1], rwkv_v0[j - 1]))
            rkv, lag, *vg = rwkv_in(x, norm_g[i, 1], rwkv_mu[j], w_rkv_all, j, loras, seq)
            if j == 0:
                rkv_first = rkv
            vmix = (rkv_first, vg[0]) if vg else None
            y = wkv7(rkv, lag, vmix, row(rwkv_k_k[j]), row(rwkv_k_a[j]),
                     row(rwkv_r_k[j]), row(rwkv_gn_g[j]), row(rwkv_gn_b[j]), seq)
            x = fused_matmul([y, (lag, 2)], w_ro_all, [(x,) + tok_tile(d)],
                             lambda y_, g_: y_ * g_, lambda acc, res: res + acc,
                             1, d, tm=tm, tn=d, name="rwkv_out_proj", lead=(j,))
        else:
            j = i - n_a
            q = fused_matmul(
                [x, row(norm_g[i, 1])], w_q_all,
                [(cos, (tm, LANES), lambda i_, j_: (i_, 0)),
                 (sin_signed, (tm, LANES), lambda i_, j_: (i_, 0))],
                _rms, _rope, 1, d, tm=tm, tn=d, name="attn_q_proj", lead=(j,))
            o = swa_sink_attention(q, kv, attn_sinks[j], seq)
            x = fused_matmul([o], w_o_all, [(x,) + tok_tile(d)],
                             ident, lambda acc, res: res + acc,
                             1, d, tm=tm, tn=d, name="attn_out_proj", lead=(j,))

        x = ffn_half_step(x, row(norm_g[i, 2]), wg_all, wu_all, wd_all, (i, 1))

        last = i == depth - 1
        fin = row(final_norm_g)

        def ple_epilogue(acc, res, p_tile, w_up, fin_g, last=last):
            out = res + _sigmoid(acc) * _dot(p_tile.astype(BF16), w_up)
            return _rms(out, fin_g) if last else out

        x = fused_matmul(
            [x, row(norm_g[i, 3])], w_pg_all,
            [(x,) + tok_tile(d),
             (p.reshape(depth, t, -1), (None, tm, p.shape[-1]), lambda i_, j_, i=i: (i, i_, 0)),
             (ple_w_up[i].astype(BF16), (p.shape[-1], d), lambda i_, j_: (0, 0)),
             (fin, (1, d), lambda i_, j_: (0, 0))],
            _rms, ple_epilogue, 1, d, tm=tm, tn=d, name="ple", lead=(i,))
    return x.reshape(bsz, seq, d)
```

```python
import functools
import math

import jax
import jax.numpy as jnp
from jax import lax
from jax.experimental import pallas as pl
from jax.experimental.pallas import tpu as pltpu

F32 = jnp.float32
BF16 = jnp.bfloat16

HEAD_DIM = 64
LANES = 128
CHUNK = 64
WINDOW = 128
GROUP = 4
ROPE_THETA = 10000.0
RMS_EPS = 1e-6
GN_EPS = 64e-5
DECAY_SCALE = math.exp(-0.5)
VMEM_LIMIT = 56 * 1024 * 1024


def _params(n_axes, **kw):
    return pltpu.CompilerParams(dimension_semantics=("arbitrary",) * n_axes,
                                vmem_limit_bytes=VMEM_LIMIT, **kw)


def _rms(x, g):
    return x * lax.rsqrt(jnp.mean(x * x, axis=-1, keepdims=True) + RMS_EPS) * g


def _sigmoid(x):
    return 1.0 / (1.0 + jnp.exp(-x))


def _dot(a, b):
    return jnp.dot(a.astype(BF16), b.astype(BF16), preferred_element_type=F32)


def _dot_nt(a, b):
    return lax.dot_general(a.astype(BF16), b.astype(BF16), (((1,), (1,)), ((), ())),
                           preferred_element_type=F32)


def _dot_tn(a, b):
    return lax.dot_general(a.astype(BF16), b.astype(BF16), (((0,), (0,)), ((), ())),
                           preferred_element_type=F32)


def _fmm_kernel(*refs, n_row, n_extra, n_out, prologue, epilogue):
    row = refs[:n_row]
    w_ref = refs[n_row]
    extra = refs[n_row + 1:n_row + 1 + n_extra]
    outs = refs[n_row + 1 + n_extra:n_row + 1 + n_extra + n_out]
    lhs_ref = refs[-1]

    @pl.when(pl.program_id(1) == 0)
    def _():
        lhs_ref[...] = prologue(*[r[...] for r in row]).astype(BF16)

    acc = _dot(lhs_ref[...], w_ref[...])
    res = epilogue(acc, *[e[...] for e in extra])
    if n_out == 1:
        res = (res,)
    for o, v in zip(outs, res):
        o[...] = v


def fused_matmul(rows, w, extras, prologue, epilogue, n_out, out_width, *, tm, tn, name, lead=()):
    k, n = w.shape[-2:]
    in_specs = []
    row_args = []
    for r in rows:
        if isinstance(r, tuple):
            arr, part = r
            in_specs.append(pl.BlockSpec((tm, k), lambda i, j, part=part: (i, part)))
        elif r.shape[0] == 1:
            arr = r
            in_specs.append(pl.BlockSpec((1, r.shape[1]), lambda i, j: (0, 0)))
        else:
            arr = r
            in_specs.append(pl.BlockSpec((tm, r.shape[1]), lambda i, j: (i, 0)))
        row_args.append(arr)
    rows = row_args
    t = max(r.shape[0] for r in rows)
    grid = (t // tm, n // tn)
    w_block = (None,) * len(lead) + (k, tn)
    if tn == n:
        in_specs.append(pl.BlockSpec(w_block, lambda i, j: lead + (0, 0),
                                     pipeline_mode=pl.Buffered(1)))
    else:
        in_specs.append(pl.BlockSpec(w_block, lambda i, j: lead + (0, j)))
    for arr, bs, im in extras:
        in_specs.append(pl.BlockSpec(bs, im))
    otn = out_width // (n // tn)
    out_shape = [jax.ShapeDtypeStruct((t, out_width), F32)] * n_out
    out_specs = [pl.BlockSpec((tm, otn), lambda i, j: (i, j))] * n_out
    res = pl.pallas_call(
        functools.partial(_fmm_kernel, n_row=len(rows), n_extra=len(extras), n_out=n_out,
                          prologue=prologue, epilogue=epilogue),
        out_shape=out_shape, grid=grid, in_specs=in_specs, out_specs=out_specs,
        scratch_shapes=[pltpu.VMEM((tm, k), BF16)],
        compiler_params=_params(2), name=name,
    )(*rows, w, *[e[0] for e in extras])
    return res[0] if n_out == 1 else res


def _ffn_kernel(x_ref, g_ref, wg_ref, wu_ref, wd_ref, o_ref, h_ref):
    f = pl.program_id(1)

    @pl.when(f == 0)
    def _():
        x = x_ref[...]
        h_ref[...] = _rms(x, g_ref[...]).astype(BF16)
        o_ref[...] = x

    h = h_ref[...]
    gate = _dot(h, wg_ref[...])
    up = _dot(h, wu_ref[...])
    act = (gate * _sigmoid(gate) * up).astype(BF16)
    o_ref[...] += _dot(act, wd_ref[...])


def ffn_half_step(x, g, wg, wu, wd_half, lead, *, tm=1024, tf=512):
    t, d = x.shape
    f = wg.shape[-1]
    none = (None,) * len(lead)
    return pl.pallas_call(
        _ffn_kernel,
        out_shape=jax.ShapeDtypeStruct((t, d), F32),
        grid=(t // tm, f // tf),
        in_specs=[pl.BlockSpec((tm, d), lambda i, j: (i, 0)),
                  pl.BlockSpec((1, d), lambda i, j: (0, 0)),
                  pl.BlockSpec(none + (d, tf), lambda i, j: lead + (0, j)),
                  pl.BlockSpec(none + (d, tf), lambda i, j: lead + (0, j)),
                  pl.BlockSpec(none + (tf, d), lambda i, j: lead + (j, 0))],
        out_specs=pl.BlockSpec((tm, d), lambda i, j: (i, 0)),
        scratch_shapes=[pltpu.VMEM((tm, d), BF16)],
        compiler_params=_params(2), name="ffn_half_step",
    )(x, g, wg, wu, wd_half)


def _rwkv_in_kernel(*refs, tiles_per_seq, n_col, has_vgate):
    x_ref, g_ref, mu_ref, w_ref, w1w_ref, w1a_ref, w1g_ref = refs[:7]
    rest = refs[7:]
    if has_vgate:
        w1v_ref, w2_ref, b_ref, w2v_ref, bv_ref, rkv_ref, lag_ref, vg_ref = rest[:8]
        xm_ref, carry_ref, mid_ref, midv_ref = rest[8:]
    else:
        w2_ref, b_ref, rkv_ref, lag_ref = rest[:4]
        xm_ref, carry_ref, mid_ref = rest[4:]
    i = pl.program_id(0)
    j = pl.program_id(1)

    @pl.when((i == 0) & (j == 0))
    def _():
        carry_ref[...] = jnp.zeros_like(carry_ref)

    @pl.when(j == 0)
    def _():
        h = _rms(x_ref[...], g_ref[...])
        tm = h.shape[0]
        last = carry_ref[7:8, :]
        first = jnp.where(i % tiles_per_seq == 0, jnp.zeros_like(last), last)
        row = lax.broadcasted_iota(jnp.int32, h.shape, 0)
        dx = jnp.where(row == 0, first, pltpu.roll(h, 1, axis=0)) - h
        carry_ref[...] = h[tm - 8:, :]

        def mix(c):
            return (h + dx * mu_ref[c:c + 1, :]).astype(BF16)

        xm_ref[0] = mix(0)
        xm_ref[1] = mix(2)
        xv = mix(3)
        xm_ref[2] = xv
        mid_ref[0] = jnp.tanh(_dot(mix(1), w1w_ref[...])).astype(BF16)
        mid_ref[1] = _dot(mix(4), w1a_ref[...]).astype(BF16)
        mid_ref[2] = _sigmoid(_dot(mix(5), w1g_ref[...])).astype(BF16)
        if has_vgate:
            midv_ref[...] = _dot(xv, w1v_ref[...]).astype(BF16)

    c = j // n_col
    z = _dot(mid_ref[c], w2_ref[...]) + b_ref[...]
    sg = _sigmoid(z)
    lag_ref[...] = jnp.where(c == 0, -DECAY_SCALE * sg, jnp.where(c == 1, sg, z))
    rkv_ref[...] = _dot(xm_ref[c], w_ref[...])

    if has_vgate:
        @pl.when(c == 0)
        def _():
            vg_ref[...] = _sigmoid(_dot(midv_ref[...], w2v_ref[...]) + bv_ref[...])


def _pad_rank(w1, w2, rank):
    r = w1.shape[1]
    return (jnp.pad(w1, ((0, 0), (0, rank - r))).astype(BF16),
            jnp.pad(w2, ((0, rank - r), (0, 0))).astype(BF16))


def _lane_multiple(n):
    return -(-n // LANES) * LANES


def rwkv_in(x, g, mu, w_rkv, layer, loras, seq, *, tm=512, tn=1024):
    t, d = x.shape
    tn = min(tn, d)
    n_col = d // tn
    has_vgate = len(loras) == 4
    const = lambda shape: pl.BlockSpec(shape, lambda i, j: (0,) * len(shape),
                                       pipeline_mode=pl.Buffered(1))
    rank = max(_lane_multiple(w1.shape[1]) for w1, _, _ in loras[:3])
    padded = [_pad_rank(w1, w2, rank) for w1, w2, _ in loras[:3]]
    w2_all = jnp.stack([w2 for _, w2 in padded])
    b_all = jnp.stack([jnp.zeros((1, d), F32) if b is None else b.reshape(1, d)
                       for _, _, b in loras[:3]])
    args = [x, g.reshape(1, d), mu, w_rkv] + [w1 for w1, _ in padded]
    in_specs = [pl.BlockSpec((tm, d), lambda i, j: (i, 0)), const((1, d)), const((6, d)),
                pl.BlockSpec((None, None, d, tn), lambda i, j: (layer, j // n_col, 0, j % n_col))]
    in_specs += [const((d, rank))] * 3
    by_branch = lambda rows: pl.BlockSpec((None, rows, tn), lambda i, j: (j // n_col, 0, j % n_col))
    col = lambda rows: pl.BlockSpec((rows, tn), lambda i, j: (0, j % n_col))
    out_shape = [jax.ShapeDtypeStruct((t, 3 * d), F32)] * 2
    out_specs = [pl.BlockSpec((tm, tn), lambda i, j: (i, j))] * 2
    scratch = [pltpu.VMEM((3, tm, d), BF16), pltpu.VMEM((8, d), F32), pltpu.VMEM((3, tm, rank), BF16)]
    if has_vgate:
        w1v, w2v, bv = loras[3]
        rank_v = _lane_multiple(w1v.shape[1])
        w1v, w2v = _pad_rank(w1v, w2v, rank_v)
        args += [w1v, w2_all, b_all, w2v, bv.reshape(1, d)]
        in_specs += [const((d, rank_v)), by_branch(rank), by_branch(1), col(rank_v), col(1)]
        out_shape.append(jax.ShapeDtypeStruct((t, d), F32))
        out_specs.append(pl.BlockSpec((tm, tn), lambda i, j: (i, jnp.minimum(j, n_col - 1))))
        scratch.append(pltpu.VMEM((tm, rank_v), BF16))
    else:
        args += [w2_all, b_all]
        in_specs += [by_branch(rank), by_branch(1)]
    return pl.pallas_call(
        functools.partial(_rwkv_in_kernel, tiles_per_seq=seq // tm, n_col=n_col,
                          has_vgate=has_vgate),
        out_shape=out_shape,
        grid=(t // tm, 3 * n_col),
        in_specs=in_specs,
        out_specs=out_specs,
        scratch_shapes=scratch,
        compiler_params=_params(2), name="rwkv_in",
    )(*args)


def _wkv_kernel(*refs, n_pairs, n_chunks, has_vmix):
    if has_vmix:
        (r_ref, k_ref, v_ref, lw_ref, a_ref, vf_ref, vg_ref,
         kk_ref, ka_ref, rk_ref, gg_ref, gb_ref, y_ref, s_ref) = refs
    else:
        (r_ref, k_ref, v_ref, lw_ref, a_ref,
         kk_ref, ka_ref, rk_ref, gg_ref, gb_ref, y_ref, s_ref) = refs

    @pl.when(pl.program_id(2) == 0)
    def _():
        s_ref[...] = jnp.zeros_like(s_ref)

    L = CHUNK
    row = lax.broadcasted_iota(jnp.int32, (L, LANES), 0)
    col = lax.broadcasted_iota(jnp.int32, (L, LANES), 1)
    strict = (col % L) < row
    incl = (col % L) <= row
    eye2 = ((col % L) == row).astype(F32)
    tri = (lax.broadcasted_iota(jnp.int32, (L, L), 1)
           <= lax.broadcasted_iota(jnp.int32, (L, L), 0)).astype(BF16)
    tri2 = jnp.concatenate([tri, tri], axis=1)
    brow = lax.broadcasted_iota(jnp.int32, (LANES, LANES), 0)
    bcol = lax.broadcasted_iota(jnp.int32, (LANES, LANES), 1)
    same_head = (brow // HEAD_DIM) == (bcol // HEAD_DIM)
    ones_bd = same_head.astype(F32)
    head0 = col < HEAD_DIM
    zeros_bd = jnp.zeros((LANES, LANES), BF16)

    def bd(x):
        xb = x.astype(BF16)
        return jnp.where(same_head, jnp.concatenate([xb, xb], axis=0), jnp.zeros((), BF16))

    def tile(ref, c, p):
        return ref[c * L:(c + 1) * L, p * LANES:(p + 1) * LANES]

    def vec(ref, p):
        return ref[:, p * LANES:(p + 1) * LANES]

    def stacked_dot(xs, w):
        out = _dot(jnp.concatenate(xs, axis=0), w)
        return [out[i * L:(i + 1) * L] for i in range(len(xs))]

    lanes2 = lambda x, y: jnp.concatenate([x, y], axis=1)
    rows2 = lambda x, y: jnp.concatenate([x, y], axis=0)

    def chain(pairs):
        items = [(c, p) for c in range(n_chunks) for p in pairs]
        r = [tile(r_ref, c, p) for c, p in items]
        k = [tile(k_ref, c, p) for c, p in items]
        v = [tile(v_ref, c, p) for c, p in items]
        lw = [tile(lw_ref, c, p) for c, p in items]
        a = [tile(a_ref, c, p) for c, p in items]
        if has_vmix:
            v = [vi + (tile(vf_ref, c, p) - vi) * tile(vg_ref, c, p)
                 for vi, (c, p) in zip(v, items)]
        kx = [ki * vec(kk_ref, p) for ki, (c, p) in zip(k, items)]
        ss = stacked_dot([x * x for x in kx], ones_bd)
        kk = [x * lax.rsqrt(jnp.maximum(s, 1e-24)) for x, s in zip(kx, ss)]
        kh = [ki * (1.0 + (ai - 1.0) * vec(ka_ref, p)) for ki, ai, (c, p) in zip(k, a, items)]
        b_s = [x * ai for x, ai in zip(kk, a)]
        lw_hi = [x.astype(BF16) for x in lw]
        lw_lo = [(x - h.astype(F32)).astype(BF16) for x, h in zip(lw, lw_hi)]
        cum = [_dot(tri2, rows2(h, l)) for h, l in zip(lw_hi, lw_lo)]
        cum_last = [x[L - 1:L, :] for x in cum]
        e_fwd = [jnp.exp(x) for x in cum]
        e_inv = [jnp.exp(-x) for x in cum]
        e_tail = [jnp.exp(xl - x) for x, xl in zip(cum, cum_last)]
        r_t = [x * e for x, e in zip(r, e_fwd)]
        a_t = [-x * jnp.exp(c_ - l_) for x, c_, l_ in zip(kk, cum, lw)]
        b_t = [x * e for x, e in zip(b_s, e_inv)]
        k_t = [x * e for x, e in zip(kh, e_inv)]
        b_hat = [x * e for x, e in zip(b_s, e_tail)]
        k_hat = [x * e for x, e in zip(kh, e_tail)]
        a_all = [_dot_nt(rows2(at, rt),
                         jnp.concatenate([jnp.where(head0, bt, 0.0), jnp.where(head0, 0.0, bt),
                                          jnp.where(head0, kt, 0.0), jnp.where(head0, 0.0, kt)],
                                         axis=0))
                 for at, rt, bt, kt in zip(a_t, r_t, b_t, k_t)]
        a_ab = [jnp.where(strict, x[:L, :LANES], 0.0) for x in a_all]
        a_ak = [jnp.where(strict, x[:L, LANES:], 0.0) for x in a_all]
        a_rb = [jnp.where(incl, x[L:, :LANES], 0.0) for x in a_all]
        a_rk = [jnp.where(incl, x[L:, LANES:], 0.0) for x in a_all]
        v_bd = [bd(x) for x in v]
        akv = [_dot(x, w) for x, w in zip(a_ak, v_bd)]
        pw = [_dot(x, bd(x)) for x in a_ab]
        t_inv = [eye2 + x for x in a_ab]
        for _ in range(int(math.log2(L)) - 2):
            res = [_dot(p_, lanes2(bd(t_), bd(p_))) for p_, t_ in zip(pw, t_inv)]
            t_inv = [t_ + x[:, :LANES] for t_, x in zip(t_inv, res)]
            pw = [x[:, LANES:] for x in res]
        t_inv = [t_ + _dot(p_, bd(t_)) for p_, t_ in zip(pw, t_inv)]
        au = [_dot(t_, lanes2(bd(at), bd(x))) for t_, at, x in zip(t_inv, a_t, akv)]
        a2 = [x[:, :LANES] for x in au]
        u0 = [x[:, LANES:] for x in au]
        qy = [_dot(lanes2(rb, rk_),
                   rows2(lanes2(bd(a2_), bd(u0_)), lanes2(zeros_bd, vb)))
              for rb, rk_, a2_, u0_, vb in zip(a_rb, a_rk, a2, u0, v_bd)]
        q = [rt + x[:, :LANES] for rt, x in zip(r_t, qy)]
        y0 = [x[:, LANES:] for x in qy]
        rkr = stacked_dot([ri * khi * vec(rk_ref, p) for ri, khi, (c, p) in zip(r, kh, items)],
                          ones_bd)
        bonus = [x * vi for x, vi in zip(rkr, v)]

        s = [s_ref[p] for p in pairs]
        y = []
        for c in range(n_chunks):
            idx = range(c * len(pairs), (c + 1) * len(pairs))
            fs = [_dot_nt(rows2(a2[i], q[i]), sp) for sp, i in zip(s, idx)]
            u = [u0[i] + f[:L] for f, i in zip(fs, idx)]
            y += [y0[i] + f[L:] for f, i in zip(fs, idx)]
            upd = [_dot_tn(rows2(ui, v[i]), rows2(b_hat[i], k_hat[i])) for ui, i in zip(u, idx)]
            s = [sp * jnp.exp(cum_last[i]) + jnp.where(same_head, up, 0.0)
                 for sp, up, i in zip(s, upd, idx)]
        for p, sp in zip(pairs, s):
            s_ref[p] = sp

        mean = stacked_dot(y, ones_bd)
        dev = [yi - m * (1.0 / HEAD_DIM) for yi, m in zip(y, mean)]
        var = stacked_dot([x * x for x in dev], ones_bd)
        for i, (c, p) in enumerate(items):
            yn = (dev[i] * lax.rsqrt(var[i] * (1.0 / HEAD_DIM) + GN_EPS) * vec(gg_ref, p)
                  + vec(gb_ref, p))
            y_ref[c * L:(c + 1) * L, p * LANES:(p + 1) * LANES] = yn + bonus[i]

    chain(list(range(n_pairs)))


def wkv7(rkv, lag, vmix, k_k, k_a, r_k, gn_g, gn_b, seq, *, n_chunks=2):
    t = rkv.shape[0]
    d = rkv.shape[1] // 3
    n_pairs = d // LANES
    gw = n_pairs * LANES
    rows = n_chunks * CHUNK
    n_grp = d // gw
    grid = (t // seq, n_grp, seq // rows)
    steps = seq // rows
    tok_at = lambda part: pl.BlockSpec((rows, gw), lambda b, g, c: (b * steps + c, part * n_grp + g))
    tok = tok_at(0)
    vec = pl.BlockSpec((1, gw), lambda b, g, c: (0, g))
    toks = [rkv, rkv, rkv, lag, lag] + (list(vmix) if vmix is not None else [])
    tok_specs = ([tok_at(0), tok_at(1), tok_at(2), tok_at(0), tok_at(1)]
                 + ([tok_at(2), tok] if vmix is not None else []))
    vecs = [k_k, k_a, r_k, gn_g, gn_b]
    return pl.pallas_call(
        functools.partial(_wkv_kernel, n_pairs=n_pairs, n_chunks=n_chunks,
                          has_vmix=vmix is not None),
        out_shape=jax.ShapeDtypeStruct((t, d), F32),
        grid=grid,
        in_specs=tok_specs + [vec] * len(vecs),
        out_specs=tok,
        scratch_shapes=[pltpu.VMEM((n_pairs, LANES, LANES), F32)],
        compiler_params=_params(3), name="wkv7",
    )(*toks, *vecs)


def _swa_kernel(q_ref, kp_ref, kc_ref, vp_ref, vc_ref, sink_ref, o_ref, *, blocks_per_seq, n_kv):
    n = pl.program_id(0) % blocks_per_seq
    w = WINDOW
    from_prev = (lax.broadcasted_iota(jnp.int32, (w, w), 1)
                 > lax.broadcasted_iota(jnp.int32, (w, w), 0))
    prev_fill = jnp.where(n > 0, 0.0, -jnp.inf)
    lane = lax.broadcasted_iota(jnp.int32, (2 * w, LANES), 1)
    low = lane < HEAD_DIM
    low_o = lax.broadcasted_iota(jnp.int32, (w, LANES), 1) < HEAD_DIM
    pairs = range(n_kv // 2)
    tiles = [(g, half) for g in range(GROUP) for half in range(2)]

    def band(p_ref, c_ref, j):
        cols = slice(j * LANES, (j + 1) * LANES)
        b = jnp.concatenate([p_ref[:, cols], c_ref[:, cols]], axis=0).astype(BF16)
        zero = jnp.zeros((), BF16)
        return jnp.concatenate([jnp.where(low, b, zero), jnp.where(low, zero, b)], axis=0)

    k2 = [band(kp_ref, kc_ref, j) for j in pairs]
    v2 = [band(vp_ref, vc_ref, j) for j in pairs]
    q4 = [jnp.concatenate([q_ref[:, (j * GROUP + g) * LANES:(j * GROUP + g + 1) * LANES]
                           for g in range(GROUP)], axis=0) for j in pairs]
    sc = [_dot_nt(q, k) * (HEAD_DIM ** -0.5) for q, k in zip(q4, k2)]
    sink = [[sink_ref[(2 * j + half) * GROUP + g] for g, half in tiles] for j in pairs]
    def folded(x, g, half):
        r0, c0 = g * w, half * 2 * w
        return jnp.where(from_prev, x[r0:r0 + w, c0:c0 + w] + prev_fill, x[r0:r0 + w, c0 + w:c0 + 2 * w])

    def unfolded(x):
        return jnp.concatenate([jnp.where(from_prev, x, 0.0), jnp.where(from_prev, 0.0, x)], axis=1)

    s = [[folded(x, g, half) for g, half in tiles] for x in sc]
    m = [[jnp.maximum(jnp.max(x, axis=-1, keepdims=True), sk) for x, sk in zip(xs, sks)]
         for xs, sks in zip(s, sink)]
    e = [[jnp.exp(x - mx) for x, mx in zip(xs, ms)] for xs, ms in zip(s, m)]
    inv = [[1.0 / (jnp.sum(x, axis=-1, keepdims=True) + jnp.exp(sk - mx))
            for x, sk, mx in zip(xs, sks, ms)] for xs, sks, ms in zip(e, sink, m)]
    e4 = [jnp.concatenate([jnp.concatenate([unfolded(xs[2 * g]), unfolded(xs[2 * g + 1])], axis=1)
                           for g in range(GROUP)], axis=0) for xs in e]
    out = [_dot(x, v) for x, v in zip(e4, v2)]
    for j in pairs:
        for g in range(GROUP):
            scale = jnp.where(low_o, inv[j][2 * g], inv[j][2 * g + 1])
            o_ref[:, (j * GROUP + g) * LANES:(j * GROUP + g + 1) * LANES] = (
                out[j][g * w:(g + 1) * w] * scale)


def paired_head_order(n_kv):
    return [(2 * j + half) * GROUP + g
            for j in range(n_kv // 2) for g in range(GROUP) for half in range(2)]


def swa_sink_attention(q, kv, sinks, seq):
    t, dq = q.shape
    dkv = kv.shape[1] // 2
    nb = t // WINDOW
    cur = lambda i: (i, 0)
    prev = lambda i: (jnp.maximum(i - 1, 0), 0)
    cur_v = lambda i: (i, 1)
    prev_v = lambda i: (jnp.maximum(i - 1, 0), 1)
    return pl.pallas_call(
        functools.partial(_swa_kernel, blocks_per_seq=seq // WINDOW, n_kv=dkv // HEAD_DIM),
        out_shape=jax.ShapeDtypeStruct((t, dq), F32),
        grid=(nb,),
        in_specs=[pl.BlockSpec((WINDOW, dq), cur),
                  pl.BlockSpec((WINDOW, dkv), prev), pl.BlockSpec((WINDOW, dkv), cur),
                  pl.BlockSpec((WINDOW, dkv), prev_v), pl.BlockSpec((WINDOW, dkv), cur_v),
                  pl.BlockSpec(memory_space=pltpu.SMEM)],
        out_specs=pl.BlockSpec((WINDOW, dq), cur),
        compiler_params=_params(1), name="swa_sink_attention",
    )(q, kv, kv, kv, kv, sinks)


def _rope(t, cos, sin_signed):
    n = t.shape[1]
    reps = n // LANES
    cos = jnp.concatenate([cos] * reps, axis=1) if reps > 1 else cos
    sin_signed = jnp.concatenate([sin_signed] * reps, axis=1) if reps > 1 else sin_signed
    lane = lax.broadcasted_iota(jnp.int32, t.shape, 1)
    half = HEAD_DIM // 2
    partner = jnp.where(lane % HEAD_DIM < half, pltpu.roll(t, n - half, axis=1),
                        pltpu.roll(t, half, axis=1))
    return t * cos + partner * sin_signed


def _rope_tables(positions):
    inv_freq = ROPE_THETA ** (-jnp.arange(0, HEAD_DIM, 2, dtype=F32) / HEAD_DIM)
    ang = positions.astype(F32).reshape(-1, 1) * inv_freq
    cos, sin = jnp.cos(ang), jnp.sin(ang)
    reps = LANES // HEAD_DIM
    return (jnp.concatenate([cos, cos] * reps, axis=1),
            jnp.concatenate([-sin, sin] * reps, axis=1))


def kernel(x, p, positions, norm_g, ffn_w_gate, ffn_w_up, ffn_w_down, ple_w_up, ple_w_gate, rwkv_mu, rwkv_w_rkv, rwkv_w_o, rwkv_w0, rwkv_w1, rwkv_w2, rwkv_a0, rwkv_a1, rwkv_a2, rwkv_v0, rwkv_v1, rwkv_v2, rwkv_g1, rwkv_g2, rwkv_k_k, rwkv_k_a, rwkv_r_k, rwkv_gn_g, rwkv_gn_b, kv_norm_g, w_kv, attn_w_q, attn_w_o, attn_sinks, final_norm_g):
    bsz, seq, d = x.shape
    depth = norm_g.shape[0]
    n_a = rwkv_mu.shape[0]
    t = bsz * seq
    x = x.reshape(t, d)
    cos, sin_signed = _rope_tables(positions)
    row = lambda v: v.reshape(1, -1)
    ident = lambda z: z
    tm = 512
    tok_tile = lambda width: ((tm, width), lambda i, j: (i, j))

    wg_all = ffn_w_gate.astype(BF16)
    wu_all = ffn_w_up.astype(BF16)
    wd_all = (0.5 * ffn_w_down).astype(BF16)
    w_rkv_all = rwkv_w_rkv.astype(BF16)
    w_ro_all = rwkv_w_o.astype(BF16)
    w_pg_all = ple_w_gate.astype(BF16)
    n_q = attn_sinks.shape[1]
    order = jnp.asarray(paired_head_order(n_q // GROUP))
    n_b = attn_w_q.shape[0]
    w_q_all = attn_w_q.reshape(n_b, d, n_q, HEAD_DIM)[:, :, order].reshape(n_b, d, -1).astype(BF16)
    w_o_all = attn_w_o.reshape(n_b, n_q, HEAD_DIM, d)[:, order].reshape(n_b, -1, d).astype(BF16)

    rkv_first = None
    kv = None
    for i in range(depth):
        if i == n_a:
            n_kv_half = w_kv.shape[1] // 2
            kv = fused_matmul(
                [x, row(kv_norm_g)], w_kv.astype(BF16),
                [(cos, (tm, LANES), lambda i_, j: (i_, 0)),
                 (sin_signed, (tm, LANES), lambda i_, j: (i_, 0))],
                _rms,
                lambda acc, c, s: jnp.where(pl.program_id(1) == 0, _rope(acc, c, s), acc),
                1, w_kv.shape[1], tm=tm, tn=n_kv_half, name="kv_proj")

        x = ffn_half_step(x, row(norm_g[i, 0]), wg_all, wu_all, wd_all, (i, 0))

        if i < n_a:
            j = i
            loras = [(rwkv_w1[j], rwkv_w2[j], rwkv_w0[j]), (rwkv_a1[j], rwkv_a2[j], rwkv_a0[j]),
                     (rwkv_g1[j], rwkv_g2[j], None)]
            if j > 0:
                loras.append((rwkv_v1[j - 1], rwkv_v2[j - 1], rwkv_v0[j - 1]))
            rkv, lag, *vg = rwkv_in(x, norm_g[i, 1], rwkv_mu[j], w_rkv_all, j, loras, seq)
            if j == 0:
                rkv_first = rkv
            vmix = (rkv_first, vg[0]) if vg else None
            y = wkv7(rkv, lag, vmix, row(rwkv_k_k[j]), row(rwkv_k_a[j]),
                     row(rwkv_r_k[j]), row(rwkv_gn_g[j]), row(rwkv_gn_b[j]), seq)
            x = fused_matmul([y, (lag, 2)], w_ro_all, [(x,) + tok_tile(d)],
                             lambda y_, g_: y_ * g_, lambda acc, res: res + acc,
                             1, d, tm=tm, tn=d, name="rwkv_out_proj", lead=(j,))
        else:
            j = i - n_a
            q = fused_matmul(
                [x, row(norm_g[i, 1])], w_q_all,
                [(cos, (tm, LANES), lambda i_, j_: (i_, 0)),
                 (sin_signed, (tm, LANES), lambda i_, j_: (i_, 0))],
                _rms, _rope, 1, d, tm=tm, tn=d, name="attn_q_proj", lead=(j,))
            o = swa_sink_attention(q, kv, attn_sinks[j], seq)
            x = fused_matmul([o], w_o_all, [(x,) + tok_tile(d)],
                             ident, lambda acc, res: res + acc,
                             1, d, tm=tm, tn=d, name="attn_out_proj", lead=(j,))

        x = ffn_half_step(x, row(norm_g[i, 2]), wg_all, wu_all, wd_all, (i, 1))

        last = i == depth - 1
        fin = row(final_norm_g)

        def ple_epilogue(acc, res, p_tile, w_up, fin_g, last=last):
            out = res + _sigmoid(acc) * _dot(p_tile.astype(BF16), w_up)
            return _rms(out, fin_g) if last else out

        x = fused_matmul(
            [x, row(norm_g[i, 3])], w_pg_all,
            [(x,) + tok_tile(d),
             (p.reshape(depth, t, -1), (None, tm, p.shape[-1]), lambda i_, j_, i=i: (i, i_, 0)),
             (ple_w_up[i].astype(BF16), (p.shape[-1], d), lambda i_, j_: (0, 0)),
             (fin, (1, d), lambda i_, j_: (0, 0))],
            _rms, ple_epilogue, 1, d, tm=tm, tn=d, name="ple", lead=(i,))
    return x.reshape(bsz, seq, d)
```

```python
import functools
import math

import jax
import jax.numpy as jnp
from jax import lax
from jax.experimental import pallas as pl
from jax.experimental.pallas import tpu as pltpu

F32 = jnp.float32
BF16 = jnp.bfloat16

HEAD_DIM = 64
LANES = 128
CHUNK = 64
WINDOW = 128
GROUP = 4
ROPE_THETA = 10000.0
RMS_EPS = 1e-6
GN_EPS = 64e-5
DECAY_SCALE = math.exp(-0.5)
VMEM_LIMIT = 56 * 1024 * 1024


def _params(n_axes, **kw):
    return pltpu.CompilerParams(dimension_semantics=("arbitrary",) * n_axes,
                                vmem_limit_bytes=VMEM_LIMIT, **kw)


def _rms(x, g):
    return x * lax.rsqrt(jnp.mean(x * x, axis=-1, keepdims=True) + RMS_EPS) * g


def _sigmoid(x):
    return 1.0 / (1.0 + jnp.exp(-x))


def _dot(a, b):
    return jnp.dot(a.astype(BF16), b.astype(BF16), preferred_element_type=F32)


def _dot_nt(a, b):
    return lax.dot_general(a.astype(BF16), b.astype(BF16), (((1,), (1,)), ((), ())),
                           preferred_element_type=F32)


def _dot_tn(a, b):
    return lax.dot_general(a.astype(BF16), b.astype(BF16), (((0,), (0,)), ((), ())),
                           preferred_element_type=F32)


def _fmm_kernel(*refs, n_row, n_extra, n_out, prologue, epilogue):
    row = refs[:n_row]
    w_ref = refs[n_row]
    extra = refs[n_row + 1:n_row + 1 + n_extra]
    outs = refs[n_row + 1 + n_extra:n_row + 1 + n_extra + n_out]
    lhs_ref = refs[-1]

    @pl.when(pl.program_id(1) == 0)
    def _():
        lhs_ref[...] = prologue(*[r[...] for r in row]).astype(BF16)

    acc = _dot(lhs_ref[...], w_ref[...])
    res = epilogue(acc, *[e[...] for e in extra])
    if n_out == 1:
        res = (res,)
    for o, v in zip(outs, res):
        o[...] = v


def fused_matmul(rows, w, extras, prologue, epilogue, n_out, out_width, *, tm, tn, name, lead=()):
    k, n = w.shape[-2:]
    in_specs = []
    row_args = []
    for r in rows:
        if isinstance(r, tuple):
            arr, part = r
            in_specs.append(pl.BlockSpec((tm, k), lambda i, j, part=part: (i, part)))
        elif r.shape[0] == 1:
            arr = r
            in_specs.append(pl.BlockSpec((1, r.shape[1]), lambda i, j: (0, 0)))
        else:
            arr = r
            in_specs.append(pl.BlockSpec((tm, r.shape[1]), lambda i, j: (i, 0)))
        row_args.append(arr)
    rows = row_args
    t = max(r.shape[0] for r in rows)
    grid = (t // tm, n // tn)
    w_block = (None,) * len(lead) + (k, tn)
    if tn == n:
        in_specs.append(pl.BlockSpec(w_block, lambda i, j: lead + (0, 0),
                                     pipeline_mode=pl.Buffered(1)))
    else:
        in_specs.append(pl.BlockSpec(w_block, lambda i, j: lead + (0, j)))
    for arr, bs, im in extras:
        in_specs.append(pl.BlockSpec(bs, im))
    otn = out_width // (n // tn)
    out_shape = [jax.ShapeDtypeStruct((t, out_width), F32)] * n_out
    out_specs = [pl.BlockSpec((tm, otn), lambda i, j: (i, j))] * n_out
    res = pl.pallas_call(
        functools.partial(_fmm_kernel, n_row=len(rows), n_extra=len(extras), n_out=n_out,
                          prologue=prologue, epilogue=epilogue),
        out_shape=out_shape, grid=grid, in_specs=in_specs, out_specs=out_specs,
        scratch_shapes=[pltpu.VMEM((tm, k), BF16)],
        compiler_params=_params(2), name=name,
    )(*rows, w, *[e[0] for e in extras])
    return res[0] if n_out == 1 else res


def _ffn_kernel(x_ref, g_ref, wg_ref, wu_ref, wd_ref, o_ref, h_ref):
    f = pl.program_id(1)

    @pl.when(f == 0)
    def _():
        x = x_ref[...]
        h_ref[...] = _rms(x, g_ref[...]).astype(BF16)
        o_ref[...] = x

    h = h_ref[...]
    gate = _dot(h, wg_ref[...])
    up = _dot(h, wu_ref[...])
    act = (0.5 * (gate * _sigmoid(gate) * up)).astype(BF16)
    o_ref[...] += _dot(act, wd_ref[...])


def ffn_half_step(x, g, wg, wu, wd_half, lead, *, tm=1024, tf=256):
    t, d = x.shape
    f = wg.shape[-1]
    none = (None,) * len(lead)
    return pl.pallas_call(
        _ffn_kernel,
        out_shape=jax.ShapeDtypeStruct((t, d), F32),
        grid=(t // tm, f // tf),
        in_specs=[pl.BlockSpec((tm, d), lambda i, j: (i, 0)),
                  pl.BlockSpec((1, d), lambda i, j: (0, 0)),
                  pl.BlockSpec(none + (d, tf), lambda i, j: lead + (0, j)),
                  pl.BlockSpec(none + (d, tf), lambda i, j: lead + (0, j)),
                  pl.BlockSpec(none + (tf, d), lambda i, j: lead + (j, 0))],
        out_specs=pl.BlockSpec((tm, d), lambda i, j: (i, 0)),
        scratch_shapes=[pltpu.VMEM((tm, d), BF16)],
        compiler_params=_params(2), name="ffn_half_step",
    )(x, g, wg, wu, wd_half)


def _rwkv_in_kernel(*refs, tiles_per_seq, n_col, has_vgate):
    x_ref, g_ref, mu_ref, w_ref, w1w_ref, w1a_ref, w1g_ref = refs[:7]
    rest = refs[7:]
    if has_vgate:
        w1v_ref, w2_ref, b_ref, w2v_ref, bv_ref, rkv_ref, lag_ref, vg_ref = rest[:8]
        xm_ref, carry_ref, mid_ref, midv_ref = rest[8:]
    else:
        w2_ref, b_ref, rkv_ref, lag_ref = rest[:4]
        xm_ref, carry_ref, mid_ref = rest[4:]
    i = pl.program_id(0)
    j = pl.program_id(1)

    @pl.when((i == 0) & (j == 0))
    def _():
        carry_ref[...] = jnp.zeros_like(carry_ref)

    @pl.when(j == 0)
    def _():
        h = _rms(x_ref[...], g_ref[...])
        tm = h.shape[0]
        last = carry_ref[7:8, :]
        first = jnp.where(i % tiles_per_seq == 0, jnp.zeros_like(last), last)
        row = lax.broadcasted_iota(jnp.int32, h.shape, 0)
        dx = jnp.where(row == 0, first, pltpu.roll(h, 1, axis=0)) - h
        carry_ref[...] = h[tm - 8:, :]

        def mix(c):
            return (h + dx * mu_ref[c:c + 1, :]).astype(BF16)

        xm_ref[0] = mix(0)
        xm_ref[1] = mix(2)
        xv = mix(3)
        xm_ref[2] = xv
        mid_ref[0] = jnp.tanh(_dot(mix(1), w1w_ref[...])).astype(BF16)
        mid_ref[1] = _dot(mix(4), w1a_ref[...]).astype(BF16)
        mid_ref[2] = _sigmoid(_dot(mix(5), w1g_ref[...])).astype(BF16)
        if has_vgate:
            midv_ref[...] = _dot(xv, w1v_ref[...]).astype(BF16)

    c = j // n_col
    z = _dot(mid_ref[c], w2_ref[...]) + b_ref[...]
    sg = _sigmoid(z)
    lag_ref[...] = jnp.where(c == 0, -DECAY_SCALE * sg, jnp.where(c == 1, sg, z))
    rkv_ref[...] = _dot(xm_ref[c], w_ref[...])

    if has_vgate:
        @pl.when(c == 0)
        def _():
            vg_ref[...] = _sigmoid(_dot(midv_ref[...], w2v_ref[...]) + bv_ref[...])


def _pad_rank(w1, w2, rank):
    r = w1.shape[1]
    return (jnp.pad(w1, ((0, 0), (0, rank - r))).astype(BF16),
            jnp.pad(w2, ((0, rank - r), (0, 0))).astype(BF16))


def _lane_multiple(n):
    return -(-n // LANES) * LANES


def rwkv_in(x, g, mu, w_rkv, layer, loras, seq, *, tm=512, tn=1024):
    t, d = x.shape
    tn = min(tn, d)
    n_col = d // tn
    has_vgate = len(loras) == 4
    const = lambda shape: pl.BlockSpec(shape, lambda i, j: (0,) * len(shape),
                                       pipeline_mode=pl.Buffered(1))
    rank = max(_lane_multiple(w1.shape[1]) for w1, _, _ in loras[:3])
    padded = [_pad_rank(w1, w2, rank) for w1, w2, _ in loras[:3]]
    w2_all = jnp.stack([w2 for _, w2 in padded])
    b_all = jnp.stack([jnp.zeros((1, d), F32) if b is None else b.reshape(1, d)
                       for _, _, b in loras[:3]])
    args = [x, g.reshape(1, d), mu, w_rkv] + [w1 for w1, _ in padded]
    in_specs = [pl.BlockSpec((tm, d), lambda i, j: (i, 0)), const((1, d)), const((6, d)),
                pl.BlockSpec((None, None, d, tn), lambda i, j: (layer, j // n_col, 0, j % n_col))]
    in_specs += [const((d, rank))] * 3
    by_branch = lambda rows: pl.BlockSpec((None, rows, tn), lambda i, j: (j // n_col, 0, j % n_col))
    col = lambda rows: pl.BlockSpec((rows, tn), lambda i, j: (0, j % n_col))
    out_shape = [jax.ShapeDtypeStruct((t, 3 * d), F32)] * 2
    out_specs = [pl.BlockSpec((tm, tn), lambda i, j: (i, j))] * 2
    scratch = [pltpu.VMEM((3, tm, d), BF16), pltpu.VMEM((8, d), F32), pltpu.VMEM((3, tm, rank), BF16)]
    if has_vgate:
        w1v, w2v, bv = loras[3]
        rank_v = _lane_multiple(w1v.shape[1])
        w1v, w2v = _pad_rank(w1v, w2v, rank_v)
        args += [w1v, w2_all, b_all, w2v, bv.reshape(1, d)]
        in_specs += [const((d, rank_v)), by_branch(rank), by_branch(1), col(rank_v), col(1)]
        out_shape.append(jax.ShapeDtypeStruct((t, d), F32))
        out_specs.append(pl.BlockSpec((tm, tn), lambda i, j: (i, jnp.minimum(j, n_col - 1))))
        scratch.append(pltpu.VMEM((tm, rank_v), BF16))
    else:
        args += [w2_all, b_all]
        in_specs += [by_branch(rank), by_branch(1)]
    return pl.pallas_call(
        functools.partial(_rwkv_in_kernel, tiles_per_seq=seq // tm, n_col=n_col,
                          has_vgate=has_vgate),
        out_shape=out_shape,
        grid=(t // tm, 3 * n_col),
        in_specs=in_specs,
        out_specs=out_specs,
        scratch_shapes=scratch,
        compiler_params=_params(2), name="rwkv_in",
    )(*args)


def _wkv_kernel(*refs, n_pairs, n_chunks, has_vmix):
    if has_vmix:
        (r_ref, k_ref, v_ref, lw_ref, a_ref, vf_ref, vg_ref,
         kk_ref, ka_ref, rk_ref, gg_ref, gb_ref, y_ref, s_ref) = refs
    else:
        (r_ref, k_ref, v_ref, lw_ref, a_ref,
         kk_ref, ka_ref, rk_ref, gg_ref, gb_ref, y_ref, s_ref) = refs

    @pl.when(pl.program_id(2) == 0)
    def _():
        s_ref[...] = jnp.zeros_like(s_ref)

    L = CHUNK
    row = lax.broadcasted_iota(jnp.int32, (L, LANES), 0)
    col = lax.broadcasted_iota(jnp.int32, (L, LANES), 1)
    strict = (col % L) < row
    incl = (col % L) <= row
    eye2 = ((col % L) == row).astype(F32)
    tri = (lax.broadcasted_iota(jnp.int32, (L, L), 1)
           <= lax.broadcasted_iota(jnp.int32, (L, L), 0)).astype(BF16)
    tri2 = jnp.concatenate([tri, tri], axis=1)
    brow = lax.broadcasted_iota(jnp.int32, (LANES, LANES), 0)
    bcol = lax.broadcasted_iota(jnp.int32, (LANES, LANES), 1)
    same_head = (brow // HEAD_DIM) == (bcol // HEAD_DIM)
    ones_bd = same_head.astype(F32)
    head0 = col < HEAD_DIM
    zeros_bd = jnp.zeros((LANES, LANES), BF16)

    def bd(x):
        xb = x.astype(BF16)
        return jnp.where(same_head, jnp.concatenate([xb, xb], axis=0), jnp.zeros((), BF16))

    def tile(ref, c, p):
        return ref[c * L:(c + 1) * L, p * LANES:(p + 1) * LANES]

    def vec(ref, p):
        return ref[:, p * LANES:(p + 1) * LANES]

    def stacked_dot(xs, w):
        out = _dot(jnp.concatenate(xs, axis=0), w)
        return [out[i * L:(i + 1) * L] for i in range(len(xs))]

    lanes2 = lambda x, y: jnp.concatenate([x, y], axis=1)
    rows2 = lambda x, y: jnp.concatenate([x, y], axis=0)

    def chain(pairs):
        items = [(c, p) for c in range(n_chunks) for p in pairs]
        r = [tile(r_ref, c, p) for c, p in items]
        k = [tile(k_ref, c, p) for c, p in items]
        v = [tile(v_ref, c, p) for c, p in items]
        lw = [tile(lw_ref, c, p) for c, p in items]
        a = [tile(a_ref, c, p) for c, p in items]
        if has_vmix:
            v = [vi + (tile(vf_ref, c, p) - vi) * tile(vg_ref, c, p)
                 for vi, (c, p) in zip(v, items)]
        kx = [ki * vec(kk_ref, p) for ki, (c, p) in zip(k, items)]
        ss = stacked_dot([x * x for x in kx], ones_bd)
        kk = [x * lax.rsqrt(jnp.maximum(s, 1e-24)) for x, s in zip(kx, ss)]
        kh = [ki * (1.0 + (ai - 1.0) * vec(ka_ref, p)) for ki, ai, (c, p) in zip(k, a, items)]
        b_s = [x * ai for x, ai in zip(kk, a)]
        lw_hi = [x.astype(BF16) for x in lw]
        lw_lo = [(x - h.astype(F32)).astype(BF16) for x, h in zip(lw, lw_hi)]
        cum = [_dot(tri2, rows2(h, l)) for h, l in zip(lw_hi, lw_lo)]
        cum_last = [x[L - 1:L, :] for x in cum]
        e_fwd = [jnp.exp(x) for x in cum]
        e_inv = [jnp.exp(-x) for x in cum]
        e_tail = [jnp.exp(xl - x) for x, xl in zip(cum, cum_last)]
        r_t = [x * e for x, e in zip(r, e_fwd)]
        a_t = [-x * jnp.exp(c_ - l_) for x, c_, l_ in zip(kk, cum, lw)]
        b_t = [x * e for x, e in zip(b_s, e_inv)]
        k_t = [x * e for x, e in zip(kh, e_inv)]
        b_hat = [x * e for x, e in zip(b_s, e_tail)]
        k_hat = [x * e for x, e in zip(kh, e_tail)]
        a_all = [_dot_nt(rows2(at, rt),
                         jnp.concatenate([jnp.where(head0, bt, 0.0), jnp.where(head0, 0.0, bt),
                                          jnp.where(head0, kt, 0.0), jnp.where(head0, 0.0, kt)],
                                         axis=0))
                 for at, rt, bt, kt in zip(a_t, r_t, b_t, k_t)]
        a_ab = [jnp.where(strict, x[:L, :LANES], 0.0) for x in a_all]
        a_ak = [jnp.where(strict, x[:L, LANES:], 0.0) for x in a_all]
        a_rb = [jnp.where(incl, x[L:, :LANES], 0.0) for x in a_all]
        a_rk = [jnp.where(incl, x[L:, LANES:], 0.0) for x in a_all]
        v_bd = [bd(x) for x in v]
        akv = [_dot(x, w) for x, w in zip(a_ak, v_bd)]
        pw = [_dot(x, bd(x)) for x in a_ab]
        t_inv = [eye2 + x for x in a_ab]
        for _ in range(int(math.log2(L)) - 2):
            res = [_dot(p_, lanes2(bd(t_), bd(p_))) for p_, t_ in zip(pw, t_inv)]
            t_inv = [t_ + x[:, :LANES] for t_, x in zip(t_inv, res)]
            pw = [x[:, LANES:] for x in res]
        t_inv = [t_ + _dot(p_, bd(t_)) for p_, t_ in zip(pw, t_inv)]
        au = [_dot(t_, lanes2(bd(at), bd(x))) for t_, at, x in zip(t_inv, a_t, akv)]
        a2 = [x[:, :LANES] for x in au]
        u0 = [x[:, LANES:] for x in au]
        qy = [_dot(lanes2(rb, rk_),
                   rows2(lanes2(bd(a2_), bd(u0_)), lanes2(zeros_bd, vb)))
              for rb, rk_, a2_, u0_, vb in zip(a_rb, a_rk, a2, u0, v_bd)]
        q = [rt + x[:, :LANES] for rt, x in zip(r_t, qy)]
        y0 = [x[:, LANES:] for x in qy]
        rkr = stacked_dot([ri * khi * vec(rk_ref, p) for ri, khi, (c, p) in zip(r, kh, items)],
                          ones_bd)
        bonus = [x * vi for x, vi in zip(rkr, v)]

        s = [s_ref[p] for p in pairs]
        y = []
        for c in range(n_chunks):
            idx = range(c * len(pairs), (c + 1) * len(pairs))
            fs = [_dot_nt(rows2(a2[i], q[i]), sp) for sp, i in zip(s, idx)]
            u = [u0[i] + f[:L] for f, i in zip(fs, idx)]
            y += [y0[i] + f[L:] for f, i in zip(fs, idx)]
            upd = [_dot_tn(rows2(ui, v[i]), rows2(b_hat[i], k_hat[i])) for ui, i in zip(u, idx)]
            s = [sp * jnp.exp(cum_last[i]) + jnp.where(same_head, up, 0.0)
                 for sp, up, i in zip(s, upd, idx)]
        for p, sp in zip(pairs, s):
            s_ref[p] = sp

        mean = stacked_dot(y, ones_bd)
        dev = [yi - m * (1.0 / HEAD_DIM) for yi, m in zip(y, mean)]
        var = stacked_dot([x * x for x in dev], ones_bd)
        for i, (c, p) in enumerate(items):
            yn = (dev[i] * lax.rsqrt(var[i] * (1.0 / HEAD_DIM) + GN_EPS) * vec(gg_ref, p)
                  + vec(gb_ref, p))
            y_ref[c * L:(c + 1) * L, p * LANES:(p + 1) * LANES] = yn + bonus[i]

    chain(list(range(n_pairs)))


def wkv7(rkv, lag, vmix, k_k, k_a, r_k, gn_g, gn_b, seq, *, n_chunks=2):
    t = rkv.shape[0]
    d = rkv.shape[1] // 3
    n_pairs = d // LANES
    gw = n_pairs * LANES
    rows = n_chunks * CHUNK
    n_grp = d // gw
    grid = (t // seq, n_grp, seq // rows)
    steps = seq // rows
    tok_at = lambda part: pl.BlockSpec((rows, gw), lambda b, g, c: (b * steps + c, part * n_grp + g))
    tok = tok_at(0)
    vec = pl.BlockSpec((1, gw), lambda b, g, c: (0, g))
    toks = [rkv, rkv, rkv, lag, lag] + (list(vmix) if vmix is not None else [])
    tok_specs = ([tok_at(0), tok_at(1), tok_at(2), tok_at(0), tok_at(1)]
                 + ([tok_at(2), tok] if vmix is not None else []))
    vecs = [k_k, k_a, r_k, gn_g, gn_b]
    return pl.pallas_call(
        functools.partial(_wkv_kernel, n_pairs=n_pairs, n_chunks=n_chunks,
                          has_vmix=vmix is not None),
        out_shape=jax.ShapeDtypeStruct((t, d), F32),
        grid=grid,
        in_specs=tok_specs + [vec] * len(vecs),
        out_specs=tok,
        scratch_shapes=[pltpu.VMEM((n_pairs, LANES, LANES), F32)],
        compiler_params=_params(3), name="wkv7",
    )(*toks, *vecs)


def _swa_kernel(q_ref, kp_ref, kc_ref, vp_ref, vc_ref, sink_ref, o_ref, *, blocks_per_seq, n_kv):
    n = pl.program_id(0) % blocks_per_seq
    w = WINDOW
    from_prev = (lax.broadcasted_iota(jnp.int32, (w, w), 1)
                 > lax.broadcasted_iota(jnp.int32, (w, w), 0))
    prev_fill = jnp.where(n > 0, 0.0, -jnp.inf)
    lane = lax.broadcasted_iota(jnp.int32, (2 * w, LANES), 1)
    low = lane < HEAD_DIM
    low_o = lax.broadcasted_iota(jnp.int32, (w, LANES), 1) < HEAD_DIM
    pairs = range(n_kv // 2)
    tiles = [(g, half) for g in range(GROUP) for half in range(2)]

    def band(p_ref, c_ref, j):
        cols = slice(j * LANES, (j + 1) * LANES)
        b = jnp.concatenate([p_ref[:, cols], c_ref[:, cols]], axis=0).astype(BF16)
        zero = jnp.zeros((), BF16)
        return jnp.concatenate([jnp.where(low, b, zero), jnp.where(low, zero, b)], axis=0)

    k2 = [band(kp_ref, kc_ref, j) for j in pairs]
    v2 = [band(vp_ref, vc_ref, j) for j in pairs]
    q4 = [jnp.concatenate([q_ref[:, (j * GROUP + g) * LANES:(j * GROUP + g + 1) * LANES]
                           for g in range(GROUP)], axis=0) for j in pairs]
    sc = [_dot_nt(q, k) * (HEAD_DIM ** -0.5) for q, k in zip(q4, k2)]
    sink = [[sink_ref[(2 * j + half) * GROUP + g] for g, half in tiles] for j in pairs]
    def folded(x, g, half):
        r0, c0 = g * w, half * 2 * w
        return jnp.where(from_prev, x[r0:r0 + w, c0:c0 + w] + prev_fill, x[r0:r0 + w, c0 + w:c0 + 2 * w])

    def unfolded(x):
        return jnp.concatenate([jnp.where(from_prev, x, 0.0), jnp.where(from_prev, 0.0, x)], axis=1)

    s = [[folded(x, g, half) for g, half in tiles] for x in sc]
    m = [[jnp.maximum(jnp.max(x, axis=-1, keepdims=True), sk) for x, sk in zip(xs, sks)]
         for xs, sks in zip(s, sink)]
    e = [[jnp.exp(x - mx) for x, mx in zip(xs, ms)] for xs, ms in zip(s, m)]
    inv = [[1.0 / (jnp.sum(x, axis=-1, keepdims=True) + jnp.exp(sk - mx))
            for x, sk, mx in zip(xs, sks, ms)] for xs, sks, ms in zip(e, sink, m)]
    e4 = [jnp.concatenate([jnp.concatenate([unfolded(xs[2 * g]), unfolded(xs[2 * g + 1])], axis=1)
                           for g in range(GROUP)], axis=0) for xs in e]
    out = [_dot(x, v) for x, v in zip(e4, v2)]
    for j in pairs:
        for g in range(GROUP):
            scale = jnp.where(low_o, inv[j][2 * g], inv[j][2 * g + 1])
            o_ref[:, (j * GROUP + g) * LANES:(j * GROUP + g + 1) * LANES] = (
                out[j][g * w:(g + 1) * w] * scale)


def paired_head_order(n_kv):
    return [(2 * j + half) * GROUP + g
            for j in range(n_kv // 2) for g in range(GROUP) for half in range(2)]


def swa_sink_attention(q, kv, sinks, seq):
    t, dq = q.shape
    dkv = kv.shape[1] // 2
    nb = t // WINDOW
    cur = lambda i: (i, 0)
    prev = lambda i: (jnp.maximum(i - 1, 0), 0)
    cur_v = lambda i: (i, 1)
    prev_v = lambda i: (jnp.maximum(i - 1, 0), 1)
    return pl.pallas_call(
        functools.partial(_swa_kernel, blocks_per_seq=seq // WINDOW, n_kv=dkv // HEAD_DIM),
        out_shape=jax.ShapeDtypeStruct((t, dq), F32),
        grid=(nb,),
        in_specs=[pl.BlockSpec((WINDOW, dq), cur),
                  pl.BlockSpec((WINDOW, dkv), prev), pl.BlockSpec((WINDOW, dkv), cur),
                  pl.BlockSpec((WINDOW, dkv), prev_v), pl.BlockSpec((WINDOW, dkv), cur_v),
                  pl.BlockSpec(memory_space=pltpu.SMEM)],
        out_specs=pl.BlockSpec((WINDOW, dq), cur),
        compiler_params=_params(1), name="swa_sink_attention",
    )(q, kv, kv, kv, kv, sinks)


def _rope(t, cos, sin_signed):
    n = t.shape[1]
    reps = n // LANES
    cos = jnp.concatenate([cos] * reps, axis=1) if reps > 1 else cos
    sin_signed = jnp.concatenate([sin_signed] * reps, axis=1) if reps > 1 else sin_signed
    lane = lax.broadcasted_iota(jnp.int32, t.shape, 1)
    half = HEAD_DIM // 2
    partner = jnp.where(lane % HEAD_DIM < half, pltpu.roll(t, n - half, axis=1),
                        pltpu.roll(t, half, axis=1))
    return t * cos + partner * sin_signed


def _rope_tables(positions):
    inv_freq = ROPE_THETA ** (-jnp.arange(0, HEAD_DIM, 2, dtype=F32) / HEAD_DIM)
    ang = positions.astype(F32).reshape(-1, 1) * inv_freq
    cos, sin = jnp.cos(ang), jnp.sin(ang)
    reps = LANES // HEAD_DIM
    return (jnp.concatenate([cos, cos] * reps, axis=1),
            jnp.concatenate([-sin, sin] * reps, axis=1))


def kernel(x, p, positions, norm_g, ffn_w_gate, ffn_w_up, ffn_w_down, ple_w_up, ple_w_gate, rwkv_mu, rwkv_w_rkv, rwkv_w_o, rwkv_w0, rwkv_w1, rwkv_w2, rwkv_a0, rwkv_a1, rwkv_a2, rwkv_v0, rwkv_v1, rwkv_v2, rwkv_g1, rwkv_g2, rwkv_k_k, rwkv_k_a, rwkv_r_k, rwkv_gn_g, rwkv_gn_b, kv_norm_g, w_kv, attn_w_q, attn_w_o, attn_sinks, final_norm_g):
    bsz, seq, d = x.shape
    depth = norm_g.shape[0]
    n_a = rwkv_mu.shape[0]
    t = bsz * seq
    x = x.reshape(t, d)
    cos, sin_signed = _rope_tables(positions)
    row = lambda v: v.reshape(1, -1)
    ident = lambda z: z
    tm = 512
    tok_tile = lambda width: ((tm, width), lambda i, j: (i, j))

    wg_all = ffn_w_gate
    wu_all = ffn_w_up
    wd_all = ffn_w_down
    w_rkv_all = rwkv_w_rkv.astype(BF16)
    w_ro_all = rwkv_w_o.astype(BF16)
    w_pg_all = ple_w_gate.astype(BF16)
    n_q = attn_sinks.shape[1]
    order = jnp.asarray(paired_head_order(n_q // GROUP))
    n_b = attn_w_q.shape[0]
    w_q_all = attn_w_q.reshape(n_b, d, n_q, HEAD_DIM)[:, :, order].reshape(n_b, d, -1).astype(BF16)
    w_o_all = attn_w_o.reshape(n_b, n_q, HEAD_DIM, d)[:, order].reshape(n_b, -1, d).astype(BF16)

    rkv_first = None
    kv = None
    for i in range(depth):
        if i == n_a:
            n_kv_half = w_kv.shape[1] // 2
            kv = fused_matmul(
                [x, row(kv_norm_g)], w_kv.astype(BF16),
                [(cos, (tm, LANES), lambda i_, j: (i_, 0)),
                 (sin_signed, (tm, LANES), lambda i_, j: (i_, 0))],
                _rms,
                lambda acc, c, s: jnp.where(pl.program_id(1) == 0, _rope(acc, c, s), acc),
                1, w_kv.shape[1], tm=tm, tn=n_kv_half, name="kv_proj")

        x = ffn_half_step(x, row(norm_g[i, 0]), wg_all, wu_all, wd_all, (i, 0))

        if i < n_a:
            j = i
            loras = [(rwkv_w1[j], rwkv_w2[j], rwkv_w0[j]), (rwkv_a1[j], rwkv_a2[j], rwkv_a0[j]),
                     (rwkv_g1[j], rwkv_g2[j], None)]
            if j > 0:
                loras.append((rwkv_v1[j - 1], rwkv_v2[j - 1], rwkv_v0[j - 1]))
            rkv, lag, *vg = rwkv_in(x, norm_g[i, 1], rwkv_mu[j], w_rkv_all, j, loras, seq)
            if j == 0:
                rkv_first = rkv
            vmix = (rkv_first, vg[0]) if vg else None
            y = wkv7(rkv, lag, vmix, row(rwkv_k_k[j]), row(rwkv_k_a[j]),
                     row(rwkv_r_k[j]), row(rwkv_gn_g[j]), row(rwkv_gn_b[j]), seq)
            x = fused_matmul([y, (lag, 2)], w_ro_all, [(x,) + tok_tile(d)],
                             lambda y_, g_: y_ * g_, lambda acc, res: res + acc,
                             1, d, tm=tm, tn=d, name="rwkv_out_proj", lead=(j,))
        else:
            j = i - n_a
            q = fused_matmul(
                [x, row(norm_g[i, 1])], w_q_all,
                [(cos, (tm, LANES), lambda i_, j_: (i_, 0)),
                 (sin_signed, (tm, LANES), lambda i_, j_: (i_, 0))],
                _rms, _rope, 1, d, tm=tm, tn=d, name="attn_q_proj", lead=(j,))
            o = swa_sink_attention(q, kv, attn_sinks[j], seq)
            x = fused_matmul([o], w_o_all, [(x,) + tok_tile(d)],
                             ident, lambda acc, res: res + acc,
                             1, d, tm=tm, tn=d, name="attn_out_proj", lead=(j,))

        x = ffn_half_step(x, row(norm_g[i, 2]), wg_all, wu_all, wd_all, (i, 1))

        last = i == depth - 1
        fin = row(final_norm_g)

        def ple_epilogue(acc, res, p_tile, w_up, fin_g, last=last):
            out = res + _sigmoid(acc) * _dot(p_tile.astype(BF16), w_up)
            return _rms(out, fin_g) if last else out

        x = fused_matmul(
            [x, row(norm_g[i, 3])], w_pg_all,
            [(x,) + tok_tile(d),
             (p.reshape(depth, t, -1), (None, tm, p.shape[-1]), lambda i_, j_, i=i: (i, i_, 0)),
             (ple_w_up[i].astype(BF16), (p.shape[-1], d), lambda i_, j_: (0, 0)),
             (fin, (1, d), lambda i_, j_: (0, 0))],
            _rms, ple_epilogue, 1, d, tm=tm, tn=d, name="ple", lead=(i,))
    return x.reshape(bsz, seq, d)
```
